```python
import math
import jax
import jax.numpy as jnp
from jax import lax
import numpy as np

D_MODEL = 2048
BATCH = 16
SEQ = 256
DEPTH = 2
DEC_BATCH = 8
DEC_SEQ = 2048
PAST_LEN = 512

GRID_W = 64
N_EVEN = (DEPTH + 1) // 2
N_ODD = DEPTH // 2
NORM_EPS = 1e-6

GLA_HEADS = 4
GLA_VW = D_MODEL // 2
GLA_DV = GLA_VW // GLA_HEADS
GLA_DK = GLA_DV // 2
GLA_QK = GLA_HEADS * GLA_DK
GLA_LORA = 16
GLA_GATE_NORM = 16.0
GLA_CHUNK = 64

RWKV_W = D_MODEL // 2
RWKV_N = 64
RWKV_HEADS = RWKV_W // RWKV_N
RWKV_DECAY_LORA = 96
RWKV_ICLR_LORA = 96
RWKV_GATE_LORA = 256
RWKV_LN_EPS = 64e-5

EVEN_IN = 2 * GLA_QK + 2 * GLA_VW + 3 * RWKV_W
EVEN_MIX = GLA_VW + RWKV_W

DIFF_DH = 128
DIFF_DV = 2 * DIFF_DH
DIFF_HEADS = D_MODEL // DIFF_DV
DIFF_QK = DIFF_HEADS * 2 * DIFF_DH
ODD_IN = 2 * DIFF_QK + DIFF_HEADS * DIFF_DV
ODD_MIX = DIFF_HEADS * DIFF_DV
ATTN_BLOCK = 128
ROPE_BASE = 10000.0
ROPE_NF = DIFF_DH // 4

FFN_HIDDEN = -(-8 * D_MODEL // (3 * 256)) * 256

kernel_name = 'hybrid_gla_rwkv7_diffattn_dit_step'


def _rmsnorm(x, g, eps=NORM_EPS):
    xf = x.astype(jnp.float32)
    xf = xf * lax.rsqrt(jnp.mean(xf * xf, axis=-1, keepdims=True) + eps)
    return xf.astype(x.dtype) * g


def _split_heads(x, n_heads):
    b, t, w = x.shape
    return x.reshape(b, t, n_heads, w // n_heads).transpose(0, 2, 1, 3)


def _merge_heads(x):
    b, h, t, d = x.shape
    return x.transpose(0, 2, 1, 3).reshape(b, t, h * d)


def _head_vec(p, n_heads):
    return p.reshape(n_heads, 1, -1)


def _adaln(cond, w, b):
    mod = jax.nn.silu(cond) @ w + b
    return [m[..., None, :] for m in jnp.split(mod, 6, axis=-1)]


def _swiglu(h, w1, w3, w2):
    return (jax.nn.silu(h @ w1) * (h @ w3)) @ w2


def _centred_shift(x, mu_prev, mu_next):
    zeros = jnp.zeros_like(x[:, :1])
    prev = jnp.concatenate([zeros, x[:, :-1]], axis=1)
    nxt = jnp.concatenate([x[:, 1:], zeros], axis=1)
    return x + mu_prev * (prev - x) + mu_next * (nxt - x)


def _gla_chunked(q, k, v, log_a, s0):
    b, h, t, dk = q.shape
    n = t // GLA_CHUNK
    resh = lambda z: z.reshape(b, h, n, GLA_CHUNK, z.shape[-1])
    q, k, v, la = resh(q), resh(k), resh(v), resh(log_a)
    cum = jnp.cumsum(la, axis=3)
    qe = q * jnp.exp(cum)
    ke = k * jnp.exp(-cum)
    mask = jnp.tril(jnp.ones((GLA_CHUNK, GLA_CHUNK), dtype=bool))
    att = jnp.where(mask, jnp.einsum('bhncd,bhnsd->bhncs', qe, ke), 0.0)
    o_intra = jnp.einsum('bhncs,bhnse->bhnce', att, v)
    cum_last = cum[..., -1:, :]
    ds = jnp.einsum('bhncd,bhnce->bhnde', k * jnp.exp(cum_last - cum), v)
    decay = jnp.exp(cum_last[..., 0, :])

    def step(s, inp):
        dec, d_s = inp
        return s * dec[..., None] + d_s, s

    s_final, s_start = lax.scan(step, s0, (jnp.moveaxis(decay, 2, 0), jnp.moveaxis(ds, 2, 0)))
    o_inter = jnp.einsum('bhncd,nbhde->bhnce', qe, s_start)
    return (o_intra + o_inter).reshape(b, h, t, v.shape[-1]), s_final


def _rwkv7_scan(r, w, kk, a, k, v, s0):
    def step(s, inp):
        r_t, w_t, kk_t, a_t, k_t, v_t = inp
        sa = jnp.einsum('bhvk,bhk->bhv', s, kk_t)
        s = s * w_t[..., None, :] - sa[..., :, None] * (kk_t * a_t)[..., None, :] + v_t[..., :, None] * k_t[..., None, :]
        return s, jnp.einsum('bhvk,bhk->bhv', s, r_t)

    xs = tuple(jnp.moveaxis(z, 2, 0) for z in (r, w, kk, a, k, v))
    s_final, ys = lax.scan(step, s0, xs)
    return jnp.moveaxis(ys, 0, 2), s_final


def _flip_t(z):
    return jnp.flip(z, axis=2)


def _even_mixer(h, s_gla, s_rwkv, w_in, w_out, gla_a1, gla_a2, gla_ab, gla_norm,
                rw_mu, rw_w0, rw_w1, rw_w2, rw_a0, rw_a1, rw_a2, rw_g1, rw_g2,
                rw_kk, rw_ka, rw_rk, rw_ln_g, rw_ln_b):
    f32 = jnp.float32
    proj = h @ w_in
    gq, gk, gv, gg, rkv = jnp.split(
        proj, [GLA_QK, 2 * GLA_QK, 2 * GLA_QK + GLA_VW, 2 * GLA_QK + 2 * GLA_VW], axis=-1)

    q = _split_heads(gq, GLA_HEADS).astype(f32) * GLA_DK ** -0.5
    k = _split_heads(gk, GLA_HEADS).astype(f32)
    v = _split_heads(gv, GLA_HEADS).astype(f32)
    gla_outs, gla_final = [], []
    for d in range(2):
        la = jax.nn.log_sigmoid(((h @ gla_a1[d]) @ gla_a2[d] + gla_ab[d]).astype(f32)) / GLA_GATE_NORM
        la = _split_heads(la, GLA_HEADS)
        s0 = s_gla[:, d].astype(f32)
        if d == 0:
            o, sf = _gla_chunked(q, k, v, la, s0)
        else:
            o, sf = _gla_chunked(_flip_t(q), _flip_t(k), _flip_t(v), _flip_t(la), s0)
            o = _flip_t(o)
        gla_outs.append(o)
        gla_final.append(sf)
    o_gla = _rmsnorm(gla_outs[0] + gla_outs[1], gla_norm)
    o_gla = _merge_heads(o_gla).astype(h.dtype) * jax.nn.silu(gg)

    rkv = _centred_shift(rkv, rw_mu[0], rw_mu[1])
    rr, rk, rv = jnp.split(rkv, 3, axis=-1)
    r = _split_heads(rr, RWKV_HEADS).astype(f32)
    k = _split_heads(rk, RWKV_HEADS).astype(f32)
    v = _split_heads(rv, RWKV_HEADS).astype(f32)
    kk = k * _head_vec(rw_kk, RWKV_HEADS)
    kk = kk * lax.rsqrt(jnp.sum(kk * kk, axis=-1, keepdims=True) + 1e-12)
    rw_outs, rw_final = [], []
    for d in range(2):
        dlog = (rw_w0[d] + jnp.tanh(h @ rw_w1[d]) @ rw_w2[d]).astype(f32)
        w = jnp.exp(-jnp.exp(-jax.nn.softplus(-dlog) - 0.5))
        a = jax.nn.sigmoid((rw_a0[d] + (h @ rw_a1[d]) @ rw_a2[d]).astype(f32))
        w = _split_heads(w, RWKV_HEADS)
        a = _split_heads(a, RWKV_HEADS)
        kd = k * (1.0 + (a - 1.0) * _head_vec(rw_ka, RWKV_HEADS))
        ins = (r, w, kk, a, kd, v)
        s0 = s_rwkv[:, d].astype(f32)
        if d == 0:
            y, sf = _rwkv7_scan(*ins, s0)
        else:
            y, sf = _rwkv7_scan(*[_flip_t(z) for z in ins], s0)
            y = _flip_t(y)
        rw_outs.append(y)
        rw_final.append(sf)
    y = rw_outs[0] + rw_outs[1]
    mu = jnp.mean(y, axis=-1, keepdims=True)
    var = jnp.mean((y - mu) ** 2, axis=-1, keepdims=True)
    y = (y - mu) * lax.rsqrt(var + RWKV_LN_EPS) * _head_vec(rw_ln_g, RWKV_HEADS) + _head_vec(rw_ln_b, RWKV_HEADS)
    bonus = jnp.sum(r * k * rw_rk[:, None, :], axis=-1, keepdims=True) * v
    gate = jax.nn.sigmoid(h @ rw_g1) @ rw_g2
    y_rw = _merge_heads(y + bonus).astype(h.dtype) * gate

    out = jnp.concatenate([o_gla, y_rw], axis=-1) @ w_out
    return out, jnp.stack(gla_final, axis=1), jnp.stack(rw_final, axis=1)


def _diff_qkv(h, w_in):
    q, k, v = jnp.split(h @ w_in, [DIFF_QK, 2 * DIFF_QK], axis=-1)
    return _split_heads(q, DIFF_HEADS), _split_heads(k, DIFF_HEADS), _split_heads(v, DIFF_HEADS)


def _axial_rope(x):
    f32 = jnp.float32
    b, hh, t, _ = x.shape
    rows = t // GRID_W
    row = jnp.repeat(jnp.arange(rows, dtype=f32), GRID_W)
    col = jnp.tile(jnp.arange(GRID_W, dtype=f32), rows)
    inv = jnp.power(ROPE_BASE, -jnp.arange(ROPE_NF, dtype=f32) / ROPE_NF)
    ang_r = (row[:, None] * inv)[:, None, :]
    ang_c = (col[:, None] * inv)[:, None, :]

    def rot(z, ang):
        z1, z2 = z[..., :ROPE_NF], z[..., ROPE_NF:]
        cos, sin = jnp.cos(ang), jnp.sin(ang)
        return jnp.concatenate([z1 * cos - z2 * sin, z1 * sin + z2 * cos], axis=-1)

    xf = x.reshape(b, hh, t, 2, DIFF_DH).astype(f32)
    half = DIFF_DH // 2
    out = jnp.concatenate([rot(xf[..., :half], ang_r), rot(xf[..., half:], ang_c)], axis=-1)
    return out.reshape(b, hh, t, 2 * DIFF_DH).astype(x.dtype)


def _diff_attention(q, k, v, lam):
    b, hh, tq, _ = q.shape
    nb = tq // ATTN_BLOCK
    qb = q.reshape(b, hh, nb, ATTN_BLOCK, 2, DIFF_DH).transpose(2, 0, 1, 3, 4, 5)
    k2 = k.reshape(b, hh, k.shape[2], 2, DIFF_DH)

    def block(qi):
        s = jnp.einsum('bhqmd,bhkmd->bhmqk', qi, k2).astype(jnp.float32) * DIFF_DH ** -0.5
        p = jax.nn.softmax(s, axis=-1)
        p = p[:, :, 0] - lam * p[:, :, 1]
        return jnp.einsum('bhqk,bhkd->bhqd', p.astype(v.dtype), v)

    o = lax.map(block, qb)
    return o.transpose(1, 2, 0, 3, 4).reshape(b, hh, tq, DIFF_DV)


def _diff_out(o, lam_init, subln, w_out):
    o = _rmsnorm(o, subln, eps=1e-5) * (1.0 - lam_init)
    return _merge_heads(o) @ w_out


def setup_inputs(seed: int = 0) -> dict:
    key = jax.random.key(seed)
    ks = iter(jax.random.split(key, 64))
    f32 = jnp.float32
    D = D_MODEL

    def nrm(shape, scale=1.0):
        return jax.random.normal(next(ks), shape, f32) * scale

    def gain(shape, base=1.0):
        return base + nrm(shape, 0.05)

    return {
        'x_prompt': nrm((BATCH, SEQ, D)),
        'x_sample': nrm((DEC_BATCH, DEC_SEQ, D)),
        'state_gla': nrm((DEC_BATCH, N_EVEN, 2, GLA_HEADS, GLA_DK, GLA_DV), 0.5),
        'state_rwkv': nrm((DEC_BATCH, N_EVEN, 2, RWKV_HEADS, RWKV_N, RWKV_N), 0.5),
        'cache_k': nrm((DEC_BATCH, N_ODD, DIFF_HEADS, PAST_LEN, 2 * DIFF_DH)),
        'cache_v': nrm((DEC_BATCH, N_ODD, DIFF_HEADS, PAST_LEN, DIFF_DV)),
        'c': nrm((DEC_BATCH, D)),
        'c_ctx': nrm((D,)),
        'ada_w': nrm((DEPTH, D, 6 * D), D ** -0.5),
        'ada_b': nrm((DEPTH, 6 * D), 0.02),
        'norm_mix': gain((DEPTH, D)),
        'norm_ffn': gain((DEPTH, D)),
        'norm_final': gain((D,)),
        'ffn_w1': nrm((DEPTH, D, FFN_HIDDEN), D ** -0.5),
        'ffn_w3': nrm((DEPTH, D, FFN_HIDDEN), D ** -0.5),
        'ffn_w2': nrm((DEPTH, FFN_HIDDEN, D), FFN_HIDDEN ** -0.5),
        'even_w_in': nrm((N_EVEN, D, EVEN_IN), D ** -0.5),
        'even_w_out': nrm((N_EVEN, EVEN_MIX, D), EVEN_MIX ** -0.5),
        'gla_a1': nrm((N_EVEN, 2, D, GLA_LORA), D ** -0.5),
        'gla_a2': nrm((N_EVEN, 2, GLA_LORA, GLA_QK), GLA_LORA ** -0.5),
        'gla_ab': nrm((N_EVEN, 2, GLA_QK), 0.5),
        'gla_norm': gain((N_EVEN, GLA_DV)),
        'rw_mu': jax.random.uniform(next(ks), (N_EVEN, 2, 3 * RWKV_W), f32, 0.0, 0.5),
        'rw_w0': nrm((N_EVEN, 2, RWKV_W), 0.5),
        'rw_w1': nrm((N_EVEN, 2, D, RWKV_DECAY_LORA), D ** -0.5),
        'rw_w2': nrm((N_EVEN, 2, RWKV_DECAY_LORA, RWKV_W), RWKV_DECAY_LORA ** -0.5),
        'rw_a0': nrm((N_EVEN, 2, RWKV_W), 0.5),
        'rw_a1': nrm((N_EVEN, 2, D, RWKV_ICLR_LORA), D ** -0.5),
        'rw_a2': nrm((N_EVEN, 2, RWKV_ICLR_LORA, RWKV_W), RWKV_ICLR_LORA ** -0.5),
        'rw_g1': nrm((N_EVEN, D, RWKV_GATE_LORA), D ** -0.5),
        'rw_g2': nrm((N_EVEN, RWKV_GATE_LORA, RWKV_W), RWKV_GATE_LORA ** -0.5),
        'rw_kk': gain((N_EVEN, RWKV_W), 0.85),
        'rw_ka': gain((N_EVEN, RWKV_W)),
        'rw_rk': nrm((N_EVEN, RWKV_HEADS, RWKV_N), 0.1),
        'rw_ln_g': gain((N_EVEN, RWKV_W)),
        'rw_ln_b': nrm((N_EVEN, RWKV_W), 0.02),
        'odd_w_in': nrm((N_ODD, D, ODD_IN), D ** -0.5),
        'odd_w_out': nrm((N_ODD, ODD_MIX, D), ODD_MIX ** -0.5),
        'diff_lq': nrm((N_ODD, 2, DIFF_DH), 0.1),
        'diff_lk': nrm((N_ODD, 2, DIFF_DH), 0.1),
        'diff_subln': gain((N_ODD, DIFF_DV)),
    }


def reference(x_prompt, x_sample, state_gla, state_rwkv, cache_k, cache_v, c,
              c_ctx, ada_w, ada_b, norm_mix, norm_ffn, norm_final, ffn_w1, ffn_w3, ffn_w2,
              even_w_in, even_w_out, gla_a1, gla_a2, gla_ab, gla_norm,
              rw_mu, rw_w0, rw_w1, rw_w2, rw_a0, rw_a1, rw_a2, rw_g1, rw_g2,
              rw_kk, rw_ka, rw_rk, rw_ln_g, rw_ln_b,
              odd_w_in, odd_w_out, diff_lq, diff_lk, diff_subln):
    f32 = jnp.float32

    def even_mix(h, s_g, s_r, e):
        return _even_mixer(h, s_g, s_r, even_w_in[e], even_w_out[e], gla_a1[e], gla_a2[e], gla_ab[e], gla_norm[e],
                           rw_mu[e], rw_w0[e], rw_w1[e], rw_w2[e], rw_a0[e], rw_a1[e], rw_a2[e], rw_g1[e], rw_g2[e],
                           rw_kk[e], rw_ka[e], rw_rk[e], rw_ln_g[e], rw_ln_b[e])

    def diff_lambda(o, l):
        lam_init = 0.8 - 0.6 * math.exp(-0.3 * l)
        lq = diff_lq[o].astype(f32)
        lk = diff_lk[o].astype(f32)
        lam = jnp.exp(jnp.sum(lq[0] * lk[0])) - jnp.exp(jnp.sum(lq[1] * lk[1])) + lam_init
        return lam_init, lam

    b_p = x_prompt.shape[0]
    x = x_prompt
    new_gla, new_rwkv, new_k, new_v = [], [], [], []
    for l in range(DEPTH):
        sh1, sc1, g1, sh2, sc2, g2 = _adaln(c_ctx, ada_w[l], ada_b[l])
        h = _rmsnorm(x, norm_mix[l]) * (1.0 + sc1) + sh1
        if l % 2 == 0:
            e = l // 2
            zg = jnp.zeros((b_p, 2, GLA_HEADS, GLA_DK, GLA_DV), f32)
            zr = jnp.zeros((b_p, 2, RWKV_HEADS, RWKV_N, RWKV_N), f32)
            out, sg, sr = even_mix(h, zg, zr, e)
            new_gla.append(sg.astype(x_prompt.dtype))
            new_rwkv.append(sr.astype(x_prompt.dtype))
        else:
            o = l // 2
            q, k, v = _diff_qkv(h, odd_w_in[o])
            lam_init, lam = diff_lambda(o, l)
            out = _diff_out(_diff_attention(q, k, v, lam), lam_init, diff_subln[o], odd_w_out[o])
            new_k.append(k)
            new_v.append(v)
        x = x + g1 * out
        h = _rmsnorm(x, norm_ffn[l]) * (1.0 + sc2) + sh2
        x = x + g2 * _swiglu(h, ffn_w1[l], ffn_w3[l], ffn_w2[l])
    y_prompt = _rmsnorm(x, norm_final)
    new_state_gla = jnp.stack(new_gla, axis=1)
    new_state_rwkv = jnp.stack(new_rwkv, axis=1)
    new_cache_k = jnp.stack(new_k, axis=1)
    new_cache_v = jnp.stack(new_v, axis=1)

    x = x_sample
    for l in range(DEPTH):
        sh1, sc1, g1, sh2, sc2, g2 = _adaln(c, ada_w[l], ada_b[l])
        h = _rmsnorm(x, norm_mix[l]) * (1.0 + sc1) + sh1
        if l % 2 == 0:
            e = l // 2
            out, _, _ = even_mix(h, state_gla[:, e], state_rwkv[:, e], e)
        else:
            o = l // 2
            q, k, v = _diff_qkv(h, odd_w_in[o])
            q = _axial_rope(q)
            k = _axial_rope(k)
            k_all = jnp.concatenate([k, cache_k[:, o].astype(k.dtype)], axis=2)
            v_all = jnp.concatenate([v, cache_v[:, o].astype(v.dtype)], axis=2)
            lam_init, lam = diff_lambda(o, l)
            out = _diff_out(_diff_attention(q, k_all, v_all, lam), lam_init, diff_subln[o], odd_w_out[o])
        x = x + g1 * out
        h = _rmsnorm(x, norm_ffn[l]) * (1.0 + sc2) + sh2
        x = x + g2 * _swiglu(h, ffn_w1[l], ffn_w3[l], ffn_w2[l])
    y_sample = _rmsnorm(x, norm_final)

    return (y_prompt, y_sample, new_state_gla, new_state_rwkv, new_cache_k, new_cache_v)
```

```python
import functools
import math

import jax
import jax.numpy as jnp
from jax import lax
from jax.experimental import pallas as pl
from jax.experimental.pallas import tpu as pltpu

F32 = jnp.float32
BF16 = jnp.bfloat16

V7X_VMEM_LIMIT_BYTES = 56 * 1024 * 1024
LANES = 128

NORM_EPS = 1e-6
GLA_GATE_NORM = 16.0
GLA_CHUNK = 64
RWKV_CHUNK = 64
RWKV_N = 64
RWKV_LN_EPS = 64e-5
DIFF_DH = 128
GRID_W = 64
ROPE_BASE = 10000.0
SUBLN_EPS = 1e-5


def _cparams(n_axes):
    return pltpu.CompilerParams(dimension_semantics=("arbitrary",) * n_axes,
                                vmem_limit_bytes=V7X_VMEM_LIMIT_BYTES)


def _dot(a, b, dims=(((1,), (0,)), ((), ()))):
    return lax.dot_general(a, b, dims, preferred_element_type=F32)


_NN = (((1,), (0,)), ((), ()))
_NT = (((1,), (1,)), ((), ()))
_TN = (((0,), (0,)), ((), ()))


def _split3(x):
    h = x.astype(BF16)
    r = x - h.astype(F32)
    m = r.astype(BF16)
    l = (r - m.astype(F32)).astype(BF16)
    return h, m, l


def _split2(x):
    h = x.astype(BF16)
    m = (x - h.astype(F32)).astype(BF16)
    return h, m


def _dot_hp(a, b, dims=_NN):
    ah, al = _split2(a)
    bh, bl = _split2(b)
    return _dot(ah, bh, dims) + (_dot(ah, bl, dims) + _dot(al, bh, dims))


def _dot_exact_lhs(a_bf, b, dims=_NN):
    h, m, l = _split3(b)
    return _dot(a_bf, h, dims) + (_dot(a_bf, m, dims) + _dot(a_bf, l, dims))


def _div_pow2(x, n):
    assert n & (n - 1) == 0
    return lax.shift_right_logical(x, n.bit_length() - 1)


def _sigmoid(x):
    return 1.0 / (1.0 + jnp.exp(-x))


def _log_sigmoid(z):
    return jnp.minimum(z, 0.0) - jnp.log(1.0 + jnp.exp(-jnp.abs(z)))


def _adaln_kernel(c_ref, w_ref, b_ref, o_ref):
    cs = c_ref[...]
    s = cs * _sigmoid(cs)
    o_ref[0] = _dot(s.astype(BF16), w_ref[0].astype(BF16)) + b_ref[0]


def _adaln(cond, ada_w, ada_b, tn=1024):
    depth, d, n = ada_w.shape
    nb = cond.shape[0]
    return pl.pallas_call(
        _adaln_kernel,
        grid=(depth, n // tn),
        in_specs=[pl.BlockSpec((nb, d), lambda l, j: (0, 0)),
                  pl.BlockSpec((1, d, tn), lambda l, j: (l, 0, j)),
                  pl.BlockSpec((1, 1, tn), lambda l, j: (l, 0, j))],
        out_specs=pl.BlockSpec((1, nb, tn), lambda l, j: (l, 0, j)),
        out_shape=jax.ShapeDtypeStruct((depth, nb, n), F32),
        compiler_params=_cparams(2),
        name="adaln_mod",
    )(cond, ada_w, ada_b.reshape(depth, 1, n))


def _norm_mod_kernel(x_ref, g_ref, sc_ref, sh_ref, o_ref):
    x = x_ref[0]
    xn = x * lax.rsqrt(jnp.mean(x * x, axis=-1, keepdims=True) + NORM_EPS)
    o_ref[0] = ((xn * g_ref[...]) * (1.0 + sc_ref[0]) + sh_ref[0]).astype(o_ref.dtype)


def _norm_mod(x, g, sc, sh, tt=256):
    b, t, d = x.shape
    tt = _pick(t, tt)
    per_batch = sc.shape[0] > 1
    mod_map = (lambda i, j: (i, 0, 0)) if per_batch else (lambda i, j: (0, 0, 0))
    return pl.pallas_call(
        _norm_mod_kernel,
        grid=(b, t // tt),
        in_specs=[pl.BlockSpec((1, tt, d), lambda i, j: (i, j, 0)),
                  pl.BlockSpec((1, d), lambda i, j: (0, 0)),
                  pl.BlockSpec((1, 1, d), mod_map),
                  pl.BlockSpec((1, 1, d), mod_map)],
        out_specs=pl.BlockSpec((1, tt, d), lambda i, j: (i, j, 0)),
        out_shape=jax.ShapeDtypeStruct((b, t, d), BF16),
        compiler_params=_cparams(2),
        name="norm_mod",
    )(x, g.reshape(1, d), sc, sh)


def _final_norm_kernel(x_ref, g_ref, o_ref):
    x = x_ref[0]
    xn = x * lax.rsqrt(jnp.mean(x * x, axis=-1, keepdims=True) + NORM_EPS)
    o_ref[0] = xn * g_ref[...]


def _final_norm(x, g, tt=256):
    b, t, d = x.shape
    tt = _pick(t, tt)
    return pl.pallas_call(
        _final_norm_kernel,
        grid=(b, t // tt),
        in_specs=[pl.BlockSpec((1, tt, d), lambda i, j: (i, j, 0)),
                  pl.BlockSpec((1, d), lambda i, j: (0, 0))],
        out_specs=pl.BlockSpec((1, tt, d), lambda i, j: (i, j, 0)),
        out_shape=jax.ShapeDtypeStruct((b, t, d), F32),
        compiler_params=_cparams(2),
        name="final_norm",
    )(x, g.reshape(1, d))


def _mm_kernel(x_ref, w_ref, o_ref):
    o_ref[...] = _dot(x_ref[...], w_ref[...]).astype(o_ref.dtype)


def _pick(n, pref):
    t = min(n, pref)
    while n % t:
        t //= 2
    return t


def _matmul(x, w, out_dtype=F32, tm=1024, tn=512, name="matmul"):
    b, t, k = x.shape
    n = w.shape[1]
    m = b * t
    tm, tn = _pick(m, tm), _pick(n, tn)
    out = pl.pallas_call(
        _mm_kernel,
        grid=(m // tm, n // tn),
        in_specs=[pl.BlockSpec((tm, k), lambda i, j: (i, 0)),
                  pl.BlockSpec((k, tn), lambda i, j: (0, j))],
        out_specs=pl.BlockSpec((tm, tn), lambda i, j: (i, j)),
        out_shape=jax.ShapeDtypeStruct((m, n), out_dtype),
        compiler_params=_cparams(2),
        name=name,
    )(x.reshape(m, k), w)
    return out.reshape(b, t, n)


def _mm_res_kernel(x_ref, w_ref, r_ref, g_ref, o_ref):
    o_ref[...] = r_ref[...] + g_ref[0] * _dot(x_ref[...], w_ref[...])


def _matmul_residual(x, w, res, gate, tm=1024, tn=512, name="matmul_residual"):
    b, t, k = x.shape
    n = w.shape[1]
    m = b * t
    per_batch = gate.shape[0] > 1
    tm = _pick(t if per_batch else m, tm)
    tn = _pick(n, tn)
    gate_map = (lambda i, j: ((i * tm) // t, 0, j)) if per_batch else (lambda i, j: (0, 0, j))
    out = pl.pallas_call(
        _mm_res_kernel,
        grid=(m // tm, n // tn),
        in_specs=[pl.BlockSpec((tm, k), lambda i, j: (i, 0)),
                  pl.BlockSpec((k, tn), lambda i, j: (0, j)),
                  pl.BlockSpec((tm, tn), lambda i, j: (i, j)),
                  pl.BlockSpec((1, 1, tn), gate_map)],
        out_specs=pl.BlockSpec((tm, tn), lambda i, j: (i, j)),
        out_shape=jax.ShapeDtypeStruct((m, n), F32),
        compiler_params=_cparams(2),
        name=name,
    )(x.reshape(m, k), w, res.reshape(m, n), gate)
    return out.reshape(b, t, n)


def _mm_swiglu_kernel(x_ref, w1_ref, w3_ref, o_ref):
    x = x_ref[...]
    a = _dot(x, w1_ref[...])
    o_ref[...] = ((a * _sigmoid(a)) * _dot(x, w3_ref[...])).astype(o_ref.dtype)


def _matmul_swiglu(x, w1, w3, tm=1024, tn=512):
    b, t, k = x.shape
    n = w1.shape[1]
    m = b * t
    tm, tn = _pick(m, tm), _pick(n, tn)
    out = pl.pallas_call(
        _mm_swiglu_kernel,
        grid=(m // tm, n // tn),
        in_specs=[pl.BlockSpec((tm, k), lambda i, j: (i, 0)),
                  pl.BlockSpec((k, tn), lambda i, j: (0, j)),
                  pl.BlockSpec((k, tn), lambda i, j: (0, j))],
        out_specs=pl.BlockSpec((tm, tn), lambda i, j: (i, j)),
        out_shape=jax.ShapeDtypeStruct((m, n), BF16),
        compiler_params=_cparams(2),
        name="swiglu_up",
    )(x.reshape(m, k), w1, w3)
    return out.reshape(b, t, n)


def _mm_heads_kernel(x_ref, w_ref, o_ref, *, hb, hw):
    r = _dot(x_ref[0], w_ref[...])
    for i in range(hb):
        o_ref[0, i] = r[:, i * hw:(i + 1) * hw]


def _matmul_heads(x, w, hw, hb=2, tm=1024):
    b, t, k = x.shape
    nh = w.shape[1] // hw
    tm = _pick(t, tm)
    return pl.pallas_call(
        functools.partial(_mm_heads_kernel, hb=hb, hw=hw),
        grid=(b, t // tm, nh // hb),
        in_specs=[pl.BlockSpec((1, tm, k), lambda bi, i, j: (bi, i, 0)),
                  pl.BlockSpec((k, hb * hw), lambda bi, i, j: (0, j))],
        out_specs=pl.BlockSpec((1, hb, tm, hw), lambda bi, i, j: (bi, j, i, 0)),
        out_shape=jax.ShapeDtypeStruct((b, nh, t, hw), F32),
        compiler_params=_cparams(3),
        name="qkv_heads",
    )(x, w)


def _gla_kernel(*refs, seq, chunk, dk, has_s0, want_sf):
    q_ref, k_ref, v_ref, gg_ref, l1_ref, a2_ref, ab_ref, gn_ref = refs[:8]
    pos = 8
    s0_ref = None
    if has_s0:
        s0_ref = refs[pos]
        pos += 1
    o_ref = refs[pos]
    pos += 1
    sf_ref = None
    if want_sf:
        sf_ref = refs[pos]
        pos += 1
    acc_ref = refs[pos]

    n_chunks = seq // chunk
    dv = v_ref.shape[-1]
    scale = dk ** -0.5
    row = lax.broadcasted_iota(jnp.int32, (chunk, chunk), 0)
    col = lax.broadcasted_iota(jnp.int32, (chunk, chunk), 1)

    for d in range(2):
        tri = (col <= row) if d == 0 else (col >= row)
        tri_bf = jnp.where(tri, 1.0, 0.0).astype(BF16)
        a2 = a2_ref[d].astype(BF16)
        ab = ab_ref[d]
        if has_s0:
            st0 = s0_ref[0, d, 0].T
        else:
            st0 = jnp.zeros((dv, dk), F32)

        def body(n, st, d=d, tri=tri, tri_bf=tri_bf, a2=a2, ab=ab):
            idx = n if d == 0 else n_chunks - 1 - n
            rows = pl.ds(pl.multiple_of(idx * chunk, chunk), chunk)
            q = q_ref[0, rows, :] * scale
            k = k_ref[0, rows, :]
            vb = v_ref[0, rows, :].astype(BF16)
            l1 = l1_ref[0, rows, d * LANES:(d + 1) * LANES]
            la = _log_sigmoid(_dot(l1.astype(BF16), a2) + ab) / GLA_GATE_NORM
            cum = _dot_exact_lhs(tri_bf, la)
            cl = cum[chunk - 1:chunk] if d == 0 else cum[0:1]
            qe = (q * jnp.exp(cum)).astype(BF16)
            ke = (k * jnp.exp(-cum)).astype(BF16)
            att = jnp.where(tri, _dot(qe, ke, _NT), 0.0)
            o = _dot(att.astype(BF16), vb) + _dot(qe, st.astype(BF16), _NT)
            kdec = (k * jnp.exp(cl - cum)).astype(BF16)
            st = st * jnp.exp(cl) + _dot(vb, kdec, _TN)
            if d == 0:
                acc_ref[rows, :] = o
            else:
                acc_ref[rows, :] = acc_ref[rows, :] + o
            return st

        st = lax.fori_loop(0, n_chunks, body, st0)
        if want_sf:
            sf_ref[0, d, 0] = st.T

    rb = min(seq, 256)
    gn = gn_ref[...]

    def epilogue(i, carry):
        rows = pl.ds(pl.multiple_of(i * rb, rb), rb)
        o = acc_ref[rows, :]
        on = o * lax.rsqrt(jnp.mean(o * o, axis=-1, keepdims=True) + NORM_EPS) * gn
        g = gg_ref[0, rows, :]
        o_ref[0, rows, :] = (on * (g * _sigmoid(g))).astype(o_ref.dtype)
        return carry

    lax.fori_loop(0, seq // rb, epilogue, 0)


def _gla(proj, a2p, ab, gnorm, s0, want_sf, heads, dk, dv, col_l1):
    b, t, _ = proj.shape
    has_s0 = s0 is not None
    kq = heads * dk // dk
    in_specs = [
        pl.BlockSpec((1, t, dk), lambda bi, h: (bi, 0, h)),
        pl.BlockSpec((1, t, dk), lambda bi, h: (bi, 0, heads + h)),
        pl.BlockSpec((1, t, dv), lambda bi, h: (bi, 0, (2 * heads * dk) // dv + h)),
        pl.BlockSpec((1, t, dv), lambda bi, h: (bi, 0, (2 * heads * dk) // dv + heads + h)),
        pl.BlockSpec((1, t, 2 * LANES), lambda bi, h: (bi, 0, col_l1 // (2 * LANES))),
        pl.BlockSpec((2, LANES, dk), lambda bi, h: (0, 0, h)),
        pl.BlockSpec((2, 1, dk), lambda bi, h: (0, 0, h)),
        pl.BlockSpec((1, dv), lambda bi, h: (0, 0)),
    ]
    args = [proj, proj, proj, proj, proj, a2p, ab, gnorm]
    if has_s0:
        in_specs.append(pl.BlockSpec((1, 2, 1, dk, dv), lambda bi, h: (bi, 0, h, 0, 0)))
        args.append(s0)
    out_specs = [pl.BlockSpec((1, t, dv), lambda bi, h: (bi, 0, h))]
    out_shape = [jax.ShapeDtypeStruct((b, t, heads * dv), BF16)]
    if want_sf:
        out_specs.append(pl.BlockSpec((1, 2, 1, dk, dv), lambda bi, h: (bi, 0, h, 0, 0)))
        out_shape.append(jax.ShapeDtypeStruct((b, 2, heads, dk, dv), F32))
    del kq
    res = pl.pallas_call(
        functools.partial(_gla_kernel, seq=t, chunk=GLA_CHUNK, dk=dk, has_s0=has_s0, want_sf=want_sf),
        grid=(b, heads),
        in_specs=in_specs,
        out_specs=out_specs,
        out_shape=out_shape,
        scratch_shapes=[pltpu.VMEM((t, dv), F32)],
        compiler_params=_cparams(2),
        name="gla_chunked",
    )(*args)
    return res if want_sf else (res[0], None)


def _rwkv_kernel(*refs, seq, chunk, want_sf):
    (r_ref, k_ref, v_ref, l1_ref, l1g_ref, w2_ref, a2_ref, g2_ref, w0_ref, a0_ref, mu_ref,
     kkw_ref, ka_ref, rk_ref, lng_ref, lnb_ref, s0_ref) = refs[:17]
    pos = 17
    o_ref = refs[pos]
    pos += 1
    sf_ref = None
    if want_sf:
        sf_ref = refs[pos]
        pos += 1
    rkv_s, rt_s, ys_s, g_s, ds_s = refs[pos:pos + 5]

    c = chunk
    c2 = 2 * chunk
    n_chunks = seq // c
    hn = RWKV_N

    lane = lax.broadcasted_iota(jnp.int32, (1, LANES), 1)
    head0 = lane < hn
    rowc = lax.broadcasted_iota(jnp.int32, (c, 1), 0)
    srow = lax.broadcasted_iota(jnp.int32, (c2, 1), 0)
    stack_mask = _div_pow2(srow, c) == _div_pow2(lane, hn)
    r2 = lax.broadcasted_iota(jnp.int32, (c2, c2), 0)
    q2 = lax.broadcasted_iota(jnp.int32, (c2, c2), 1)
    same_head = _div_pow2(r2, c) == _div_pow2(q2, c)
    rr = jnp.where(r2 >= c, r2 - c, r2)
    qq = jnp.where(q2 >= c, q2 - c, q2)
    eye2 = jnp.where(r2 == q2, 1.0, 0.0).astype(F32)
    lr = lax.broadcasted_iota(jnp.int32, (LANES, LANES), 0)
    lc = lax.broadcasted_iota(jnp.int32, (LANES, LANES), 1)
    diag_l = lr == lc
    block_l = _div_pow2(lr, hn) == _div_pow2(lc, hn)
    crow = lax.broadcasted_iota(jnp.int32, (c, c), 0)
    ccol = lax.broadcasted_iota(jnp.int32, (c, c), 1)

    def head_sum(x):
        s0 = jnp.sum(jnp.where(head0, x, 0.0), axis=-1, keepdims=True)
        s1 = jnp.sum(jnp.where(head0, 0.0, x), axis=-1, keepdims=True)
        return jnp.where(head0, s0, s1)

    def stack(x):
        return jnp.where(stack_mask, jnp.concatenate([x, x], axis=0), 0.0)

    def unstack(xs):
        return xs[:c] + xs[c:]

    mu = mu_ref[...]
    kkw = kkw_ref[...]
    ka = ka_ref[...]

    def shifted(ref, j, start, is_first, is_last):
        x = ref[0, pl.ds(start, c), :]
        pstart = pl.multiple_of(jnp.maximum(start - 8, 0), 8)
        nstart = pl.multiple_of(jnp.minimum(start + c, seq - 8), 8)
        prow = jnp.where(is_first, 0.0, ref[0, pl.ds(pstart, 8), :][7:8])
        nrow = jnp.where(is_last, 0.0, ref[0, pl.ds(nstart, 8), :][0:1])
        xp = jnp.where(rowc == 0, prow, pltpu.roll(x, 1, 0))
        xn = jnp.where(rowc == c - 1, nrow, pltpu.roll(x, c - 1, 0))
        return x + mu[0, j] * (xp - x) + mu[1, j] * (xn - x)

    def phase_a(n, carry):
        start = pl.multiple_of(n * c, c)
        rows = pl.ds(start, c)
        is_first = n == 0
        is_last = n == n_chunks - 1
        r = shifted(r_ref, 0, start, is_first, is_last)
        k = shifted(k_ref, 1, start, is_first, is_last)
        v = shifted(v_ref, 2, start, is_first, is_last)
        rkv_s[0, rows, :] = r
        rkv_s[1, rows, :] = k
        rkv_s[2, rows, :] = v
        kk = k * kkw
        kk = kk * lax.rsqrt(head_sum(kk * kk) + 1e-12)
        vs = stack(v)
        y0_sum = jnp.zeros((c, LANES), F32)
        for d in range(2):
            l1w = l1_ref[0, rows, d * LANES:(d + 1) * LANES]
            l1a = l1_ref[0, rows, (2 + d) * LANES:(3 + d) * LANES]
            dlog = w0_ref[d] + _dot(jnp.tanh(l1w).astype(BF16), w2_ref[d])
            sp = jnp.maximum(-dlog, 0.0) + jnp.log(1.0 + jnp.exp(-jnp.abs(dlog)))
            lw = -jnp.exp(-sp - 0.5)
            a = _sigmoid(a0_ref[d] + _dot(l1a.astype(BF16), a2_ref[d]))
            kd = k * (1.0 + (a - 1.0) * ka)
            bb = a * kk
            if d == 0:
                tri_i = ccol <= crow
                strict = qq < rr
                incl = qq <= rr
            else:
                tri_i = ccol >= crow
                strict = qq > rr
                incl = qq >= rr
            strict = jnp.logical_and(strict, same_head)
            incl = jnp.logical_and(incl, same_head)
            lcum = _dot_exact_lhs(jnp.where(tri_i, 1.0, 0.0).astype(BF16), lw)
            ltot = lcum[c - 1:c] if d == 0 else lcum[0:1]
            e_l = jnp.exp(lcum)
            e_nl = jnp.exp(-lcum)
            e_rem = jnp.exp(ltot - lcum)
            rs = stack(r * e_l)
            ks = stack(kk * jnp.exp(lcum - lw))
            kbs = stack(kd * e_nl)
            bbs = stack(bb * e_nl)
            kts = stack(kd * e_rem)
            bts = stack(bb * e_rem)

            a_kk = jnp.where(strict, _dot_hp(ks, kbs, _NT), 0.0)
            a_kb = jnp.where(strict, _dot_hp(ks, bbs, _NT), 0.0)
            a_rk = jnp.where(incl, _dot_hp(rs, kbs, _NT), 0.0)
            a_rb = jnp.where(incl, _dot_hp(rs, bbs, _NT), 0.0)

            p = -a_kb
            tinv = eye2 + p
            span = 2
            while span < c:
                p = _dot_hp(p, p)
                tinv = tinv + _dot_hp(tinv, p)
                span *= 2

            av = _dot_hp(a_kk, vs)
            ws = _dot_hp(tinv, ks)
            u0 = _dot_hp(tinv, av)
            rt = rs - _dot_hp(a_rb, ws)
            y0 = _dot_hp(a_rk, vs) - _dot_hp(a_rb, u0)
            gmat = jnp.where(diag_l, jnp.exp(ltot), 0.0) - jnp.where(block_l, _dot_hp(ws, bts, _TN), 0.0)
            dmat = jnp.where(block_l, _dot_hp(vs, kts, _TN) - _dot_hp(u0, bts, _TN), 0.0)

            rt_s[d, rows, :] = unstack(rt)
            y0_sum = y0_sum + unstack(y0)
            g_s[d, n] = gmat
            ds_s[d, n] = dmat
        ys_s[rows, :] = y0_sum
        return carry

    lax.fori_loop(0, n_chunks, phase_a, 0)

    def phase_b(i, carry):
        sf, sb = carry
        out = []
        for d, st in ((0, sf), (1, sb)):
            idx = i if d == 0 else n_chunks - 1 - i
            rows = pl.ds(pl.multiple_of(idx * c, c), c)
            ys_s[rows, :] = ys_s[rows, :] + _dot_hp(rt_s[d, rows, :], st, _NT)
            out.append(_dot_hp(st, g_s[d, idx]) + ds_s[d, idx])
        return tuple(out)

    sf, sb = lax.fori_loop(0, n_chunks, phase_b, (s0_ref[0, 0, 0], s0_ref[0, 1, 0]))
    if want_sf:
        sf_ref[0, 0, 0] = sf
        sf_ref[0, 1, 0] = sb

    rb = min(seq, 256)
    lng = lng_ref[...]
    lnb = lnb_ref[...]
    rkw = rk_ref[...]
    g2 = g2_ref[...]

    def epilogue(i, carry):
        rows = pl.ds(pl.multiple_of(i * rb, rb), rb)
        y = ys_s[rows, :]
        mean = head_sum(y) * (1.0 / hn)
        yc = y - mean
        var = head_sum(yc * yc) * (1.0 / hn)
        yn = yc * lax.rsqrt(var + RWKV_LN_EPS) * lng + lnb
        r = rkv_s[0, rows, :]
        k = rkv_s[1, rows, :]
        v = rkv_s[2, rows, :]
        bonus = head_sum(r * k * rkw) * v
        gate = _dot(_sigmoid(l1g_ref[0, rows, :]).astype(BF16), g2)
        o_ref[0, rows, :] = ((yn + bonus) * gate).astype(o_ref.dtype)
        return carry

    lax.fori_loop(0, seq // rb, epilogue, 0)


def _rwkv(proj, w2p, a2p, g2, w0, a0, mu, kkw, ka, rkw, lng, lnb, s0_bd, want_sf, col_rkv, col_l1, col_g):
    b, t, _ = proj.shape
    width = kkw.shape[-1]
    pairs = width // LANES
    c = RWKV_CHUNK
    n_chunks = t // c
    cb = col_rkv // LANES
    vec = lambda: pl.BlockSpec((1, LANES), lambda bi, h: (0, h))
    in_specs = [
        pl.BlockSpec((1, t, LANES), lambda bi, h: (bi, 0, cb + h)),
        pl.BlockSpec((1, t, LANES), lambda bi, h: (bi, 0, cb + pairs + h)),
        pl.BlockSpec((1, t, LANES), lambda bi, h: (bi, 0, cb + 2 * pairs + h)),
        pl.BlockSpec((1, t, 4 * LANES), lambda bi, h: (bi, 0, col_l1 // (4 * LANES))),
        pl.BlockSpec((1, t, 2 * LANES), lambda bi, h: (bi, 0, col_g // (2 * LANES))),
        pl.BlockSpec((2, LANES, LANES), lambda bi, h: (0, 0, h)),
        pl.BlockSpec((2, LANES, LANES), lambda bi, h: (0, 0, h)),
        pl.BlockSpec((2 * LANES, LANES), lambda bi, h: (0, h)),
        pl.BlockSpec((2, 1, LANES), lambda bi, h: (0, 0, h)),
        pl.BlockSpec((2, 1, LANES), lambda bi, h: (0, 0, h)),
        pl.BlockSpec((2, 3, 1, LANES), lambda bi, h: (0, 0, 0, h)),
        vec(), vec(), vec(), vec(), vec(),
        pl.BlockSpec((1, 2, 1, LANES, LANES), lambda bi, h: (bi, 0, h, 0, 0)),
    ]
    out_specs = [pl.BlockSpec((1, t, LANES), lambda bi, h: (bi, 0, h))]
    out_shape = [jax.ShapeDtypeStruct((b, t, width), BF16)]
    if want_sf:
        out_specs.append(pl.BlockSpec((1, 2, 1, LANES, LANES), lambda bi, h: (bi, 0, h, 0, 0)))
        out_shape.append(jax.ShapeDtypeStruct((b, 2, pairs, LANES, LANES), F32))
    res = pl.pallas_call(
        functools.partial(_rwkv_kernel, seq=t, chunk=c, want_sf=want_sf),
        grid=(b, pairs),
        in_specs=in_specs,
        out_specs=out_specs,
        out_shape=out_shape,
        scratch_shapes=[pltpu.VMEM((3, t, LANES), F32),
                        pltpu.VMEM((2, t, LANES), F32),
                        pltpu.VMEM((t, LANES), F32),
                        pltpu.VMEM((2, n_chunks, LANES, LANES), F32),
                        pltpu.VMEM((2, n_chunks, LANES, LANES), F32)],
        compiler_params=_cparams(2),
        name="rwkv7_chunked",
    )(proj, proj, proj, proj, proj, w2p, a2p, g2, w0, a0, mu, kkw, ka, rkw, lng, lnb, s0_bd)
    return res if want_sf else (res[0], None)


def _rope(x, cos, sin, first_half):
    part = jnp.where(first_half, pltpu.roll(x, x.shape[-1] - 32, 1), pltpu.roll(x, 32, 1))
    return x * cos + part * sin


def _diff_attn_kernel(*refs, rope, has_cache, lam_init, tq):
    q_ref, k_ref, v_ref, lq_ref, lk_ref, sub_ref = refs[:6]
    pos = 6
    if rope:
        cos_ref, sin_ref = refs[pos:pos + 2]
        pos += 2
    if has_cache:
        ck_ref, cv_ref = refs[pos:pos + 2]
        pos += 2
    o_ref = refs[pos]
    ks_ref, vs_ref = refs[pos + 1:pos + 3]

    t = k_ref.shape[2]
    w = k_ref.shape[3]
    dh = DIFF_DH
    i = pl.program_id(2)
    lane = lax.broadcasted_iota(jnp.int32, (1, w), 1)
    first_half = (lane & 63) < 32

    @pl.when(i == 0)
    def _():
        kb = min(t, 512)

        def fill(j, carry):
            rows = pl.ds(pl.multiple_of(j * kb, kb), kb)
            kx = k_ref[0, 0, rows, :]
            if rope:
                kx = _rope(kx, cos_ref[rows, :], sin_ref[rows, :], first_half)
            ks_ref[rows, :] = kx.astype(BF16)
            vs_ref[rows, :] = v_ref[0, 0, rows, :].astype(BF16)
            return carry

        lax.fori_loop(0, t // kb, fill, 0)
        if has_cache:
            ks_ref[t:, :] = ck_ref[0, 0, 0].astype(BF16)
            vs_ref[t:, :] = cv_ref[0, 0, 0].astype(BF16)

    q = q_ref[0, 0]
    if rope:
        qrows = pl.ds(pl.multiple_of(i * tq, tq), tq)
        q = _rope(q, cos_ref[qrows, :], sin_ref[qrows, :], first_half)
    qb = q.astype(BF16)
    lq = lq_ref[...]
    lk = lk_ref[...]
    lam = (jnp.exp(jnp.sum(lq[0:1] * lk[0:1], axis=-1, keepdims=True))
           - jnp.exp(jnp.sum(lq[1:2] * lk[1:2], axis=-1, keepdims=True)) + lam_init)
    kall = ks_ref[...]
    scale = dh ** -0.5
    probs = []
    for m in range(2):
        s = _dot(qb[:, m * dh:(m + 1) * dh], kall[:, m * dh:(m + 1) * dh], _NT) * scale
        e = jnp.exp(s - jnp.max(s, axis=-1, keepdims=True))
        probs.append(e * (1.0 / jnp.sum(e, axis=-1, keepdims=True)))
    p = probs[0] - lam * probs[1]
    o = _dot(p.astype(BF16), vs_ref[...])
    on = o * lax.rsqrt(jnp.mean(o * o, axis=-1, keepdims=True) + SUBLN_EPS) * sub_ref[...]
    o_ref[0] = (on * (1.0 - lam_init)).astype(o_ref.dtype)


def _diff_attention(qkv, heads, lq, lk, subln, lam_init, rope_tabs, cache_k, cache_v, tq=256):
    b, _, t, w = qkv.shape
    tq = _pick(t, tq)
    rope = rope_tabs is not None
    has_cache = cache_k is not None
    tk = t + (cache_k.shape[3] if has_cache else 0)
    in_specs = [
        pl.BlockSpec((1, 1, tq, w), lambda bi, h, i: (bi, h, i, 0)),
        pl.BlockSpec((1, 1, t, w), lambda bi, h, i: (bi, heads + h, 0, 0)),
        pl.BlockSpec((1, 1, t, w), lambda bi, h, i: (bi, 2 * heads + h, 0, 0)),
        pl.BlockSpec((2, DIFF_DH), lambda bi, h, i: (0, 0)),
        pl.BlockSpec((2, DIFF_DH), lambda bi, h, i: (0, 0)),
        pl.BlockSpec((1, w), lambda bi, h, i: (0, 0)),
    ]
    args = [qkv, qkv, qkv, lq, lk, subln.reshape(1, w)]
    if rope:
        in_specs += [pl.BlockSpec((t, w), lambda bi, h, i: (0, 0))] * 2
        args += list(rope_tabs)
    if has_cache:
        pc = cache_k.shape[3]
        in_specs += [pl.BlockSpec((1, 1, 1, pc, w), lambda bi, h, i: (bi, 0, h, 0, 0))] * 2
        args += [cache_k, cache_v]
    return pl.pallas_call(
        functools.partial(_diff_attn_kernel, rope=rope, has_cache=has_cache, lam_init=lam_init, tq=tq),
        grid=(b, heads, t // tq),
        in_specs=in_specs,
        out_specs=pl.BlockSpec((1, tq, w), lambda bi, h, i: (bi, i, h)),
        out_shape=jax.ShapeDtypeStruct((b, t, heads * w), BF16),
        scratch_shapes=[pltpu.VMEM((tk, w), BF16), pltpu.VMEM((tk, w), BF16)],
        compiler_params=_cparams(3),
        name="diff_attention",
    )(*args)


def _rope_tables(t, w):
    nf = DIFF_DH // 4
    rows = t // GRID_W
    row = jnp.repeat(jnp.arange(rows, dtype=F32), GRID_W)
    col = jnp.tile(jnp.arange(GRID_W, dtype=F32), rows)
    inv = jnp.power(ROPE_BASE, -jnp.arange(nf, dtype=F32) / nf)
    ang_r = row[:, None] * inv
    ang_c = col[:, None] * inv
    cos = jnp.concatenate([jnp.cos(ang_r), jnp.cos(ang_r), jnp.cos(ang_c), jnp.cos(ang_c)], axis=-1)
    sin = jnp.concatenate([-jnp.sin(ang_r), jnp.sin(ang_r), -jnp.sin(ang_c), jnp.sin(ang_c)], axis=-1)
    reps = w // DIFF_DH
    return jnp.tile(cos, (1, reps)), jnp.tile(sin, (1, reps))


def _pad_rows(w, rows):
    return jnp.pad(w, ((0, 0),) * (w.ndim - 2) + ((0, rows - w.shape[-2]), (0, 0)))


def _pad_cols(w, cols):
    return jnp.pad(w, ((0, 0),) * (w.ndim - 1) + ((0, cols - w.shape[-1]),))


def kernel(x_prompt, x_sample, state_gla, state_rwkv, cache_k, cache_v, c, c_ctx, ada_w, ada_b, norm_mix, norm_ffn, norm_final, ffn_w1, ffn_w3, ffn_w2, even_w_in, even_w_out, gla_a1, gla_a2, gla_ab, gla_norm, rw_mu, rw_w0, rw_w1, rw_w2, rw_a0, rw_a1, rw_a2, rw_g1, rw_g2, rw_kk, rw_ka, rw_rk, rw_ln_g, rw_ln_b, odd_w_in, odd_w_out, diff_lq, diff_lk, diff_subln):
    depth, d, _ = ada_w.shape
    dec_b = x_sample.shape[0]
    gla_heads, gla_dk, gla_dv = state_gla.shape[3:]
    rw_heads = state_rwkv.shape[3]
    rw_w = rw_heads * RWKV_N
    pairs = rw_w // LANES
    diff_heads, _, diff_w = cache_k.shape[2:]
    gla_qk = gla_heads * gla_dk
    gla_vw = gla_heads * gla_dv
    even_in = even_w_in.shape[-1]

    nb = -(-(1 + dec_b) // 8) * 8
    cond = jnp.zeros((nb, d), F32).at[0].set(c_ctx).at[1:1 + dec_b].set(c)
    mod = _adaln(cond, ada_w, ada_b)

    def mods(layer, prompt):
        m = mod[layer, 0:1] if prompt else mod[layer, 1:1 + dec_b]
        return [m[:, None, k * d:(k + 1) * d] for k in range(6)]

    def even_weights(e):
        lora1 = jnp.concatenate(
            [_pad_cols(rw_w1[e, 0], LANES), _pad_cols(rw_w1[e, 1], LANES),
             _pad_cols(rw_a1[e, 0], LANES), _pad_cols(rw_a1[e, 1], LANES),
             rw_g1[e],
             _pad_cols(gla_a1[e, 0], LANES), _pad_cols(gla_a1[e, 1], LANES)], axis=-1)
        w_cat = jnp.concatenate([even_w_in[e], lora1], axis=-1).astype(BF16)
        return dict(
            w_cat=w_cat,
            w_out=even_w_out[e].astype(BF16),
            gla_a2=_pad_rows(gla_a2[e], LANES),
            gla_ab=gla_ab[e].reshape(2, 1, gla_qk),
            gla_norm=gla_norm[e].reshape(1, gla_dv),
            w2=_pad_rows(rw_w2[e], LANES).astype(BF16),
            a2=_pad_rows(rw_a2[e], LANES).astype(BF16),
            g2=rw_g2[e].astype(BF16),
            w0=rw_w0[e].reshape(2, 1, rw_w),
            a0=rw_a0[e].reshape(2, 1, rw_w),
            mu=rw_mu[e].reshape(2, 3, 1, rw_w),
            kk=rw_kk[e].reshape(1, rw_w), ka=rw_ka[e].reshape(1, rw_w), rk=rw_rk[e].reshape(1, rw_w),
            lng=rw_ln_g[e].reshape(1, rw_w), lnb=rw_ln_b[e].reshape(1, rw_w),
        )

    col_rkv = 2 * gla_qk + 2 * gla_vw
    col_l1 = even_in
    col_g = even_in + 4 * LANES
    col_gl = even_in + 6 * LANES

    def block_diag_state(s):
        bsz = s.shape[0]
        s = s.reshape(bsz, 2, pairs, 2, RWKV_N, RWKV_N)
        z = jnp.zeros_like(s[:, :, :, 0])
        top = jnp.concatenate([s[:, :, :, 0], z], axis=-1)
        bot = jnp.concatenate([z, s[:, :, :, 1]], axis=-1)
        return jnp.concatenate([top, bot], axis=-2)

    def diag_blocks(sbd):
        bsz = sbd.shape[0]
        h0 = sbd[:, :, :, :RWKV_N, :RWKV_N]
        h1 = sbd[:, :, :, RWKV_N:, RWKV_N:]
        return jnp.stack([h0, h1], axis=3).reshape(bsz, 2, rw_heads, RWKV_N, RWKV_N)

    def even_mixer(h, ew, s_gla, s_rwkv, want_states):
        proj = _matmul(h, ew["w_cat"], name="even_in_proj")
        o_gla, sg = _gla(proj, ew["gla_a2"], ew["gla_ab"], ew["gla_norm"], s_gla, want_states,
                         gla_heads, gla_dk, gla_dv, col_gl)
        if s_rwkv is None:
            s_bd = jnp.zeros((h.shape[0], 2, pairs, LANES, LANES), F32)
        else:
            s_bd = block_diag_state(s_rwkv)
        y_rw, sr = _rwkv(proj, ew["w2"], ew["a2"], ew["g2"], ew["w0"], ew["a0"], ew["mu"], ew["kk"], ew["ka"],
                         ew["rk"], ew["lng"], ew["lnb"], s_bd, want_states, col_rkv, col_l1, col_g)
        mix = jnp.concatenate([o_gla, y_rw], axis=-1)
        return mix, ew["w_out"], sg, (diag_blocks(sr) if want_states else None)

    def run_group(x, prompt):
        outs = {}
        for layer in range(depth):
            sh1, sc1, g1, sh2, sc2, g2 = mods(layer, prompt)
            h = _norm_mod(x, norm_mix[layer], sc1, sh1)
            if layer % 2 == 0:
                e = layer // 2
                ew = even_weights(e)
                if prompt:
                    mix, w_out, sg, sr = even_mixer(h, ew, None, None, True)
                    outs.setdefault("gla", []).append(sg)
                    outs.setdefault("rwkv", []).append(sr)
                else:
                    mix, w_out, _, _ = even_mixer(h, ew, state_gla[:, e], state_rwkv[:, e], False)
            else:
                o = layer // 2
                lam_init = 0.8 - 0.6 * math.exp(-0.3 * layer)
                qkv = _matmul_heads(h, odd_w_in[o].astype(BF16), diff_w)
                if prompt:
                    outs.setdefault("k", []).append(qkv[:, diff_heads:2 * diff_heads])
                    outs.setdefault("v", []).append(qkv[:, 2 * diff_heads:])
                    mix = _diff_attention(qkv, diff_heads, diff_lq[o], diff_lk[o], diff_subln[o], lam_init,
                                          None, None, None)
                else:
                    mix = _diff_attention(qkv, diff_heads, diff_lq[o], diff_lk[o], diff_subln[o], lam_init,
                                          _rope_tables(x.shape[1], diff_w), cache_k[:, o:o + 1], cache_v[:, o:o + 1])
                w_out = odd_w_out[o].astype(BF16)
            x = _matmul_residual(mix, w_out, x, g1, name="mix_out_proj")
            h = _norm_mod(x, norm_ffn[layer], sc2, sh2)
            u = _matmul_swiglu(h, ffn_w1[layer].astype(BF16), ffn_w3[layer].astype(BF16))
            x = _matmul_residual(u, ffn_w2[layer].astype(BF16), x, g2, name="ffn_down_proj")
        return _final_norm(x, norm_final), outs

    y_prompt, po = run_group(x_prompt, True)
    y_sample, _ = run_group(x_sample, False)
    return (y_prompt, y_sample,
            jnp.stack(po["gla"], axis=1), jnp.stack(po["rwkv"], axis=1),
            jnp.stack(po["k"], axis=1), jnp.stack(po["v"], axis=1))
```

```python
import functools
import math

import jax
import jax.numpy as jnp
from jax import lax
from jax.experimental import pallas as pl
from jax.experimental.pallas import tpu as pltpu

F32 = jnp.float32
BF16 = jnp.bfloat16

V7X_VMEM_LIMIT_BYTES = 56 * 1024 * 1024
LANES = 128

NORM_EPS = 1e-6
GLA_GATE_NORM = 16.0
GLA_CHUNK = 64
RWKV_CHUNK = 64
RWKV_N = 64
RWKV_LN_EPS = 64e-5
DIFF_DH = 128
GRID_W = 64
ROPE_BASE = 10000.0
SUBLN_EPS = 1e-5


def _cparams(n_axes):
    return pltpu.CompilerParams(dimension_semantics=("arbitrary",) * n_axes,
                                vmem_limit_bytes=V7X_VMEM_LIMIT_BYTES)


def _dot(a, b, dims=(((1,), (0,)), ((), ()))):
    return lax.dot_general(a, b, dims, preferred_element_type=F32)


_NN = (((1,), (0,)), ((), ()))
_NT = (((1,), (1,)), ((), ()))
_TN = (((0,), (0,)), ((), ()))


def _split3(x):
    h = x.astype(BF16)
    r = x - h.astype(F32)
    m = r.astype(BF16)
    l = (r - m.astype(F32)).astype(BF16)
    return h, m, l


def _split2(x):
    h = x.astype(BF16)
    m = (x - h.astype(F32)).astype(BF16)
    return h, m


def _dot_hp(a, b, dims=_NN):
    ah, al = _split2(a)
    bh, bl = _split2(b)
    return _dot(ah, bh, dims) + (_dot(ah, bl, dims) + _dot(al, bh, dims))


def _dot_exact_lhs(a_bf, b, dims=_NN):
    h, m, l = _split3(b)
    return _dot(a_bf, h, dims) + (_dot(a_bf, m, dims) + _dot(a_bf, l, dims))


def _div_pow2(x, n):
    assert n & (n - 1) == 0
    return lax.shift_right_logical(x, n.bit_length() - 1)


def _sigmoid(x):
    return 1.0 / (1.0 + jnp.exp(-x))


def _log_sigmoid(z):
    return jnp.minimum(z, 0.0) - jnp.log(1.0 + jnp.exp(-jnp.abs(z)))


def _adaln_kernel(c_ref, w_ref, b_ref, o_ref):
    cs = c_ref[...]
    s = cs * _sigmoid(cs)
    o_ref[0] = _dot(s.astype(BF16), w_ref[0].astype(BF16)) + b_ref[0]


def _adaln(cond, ada_w, ada_b, tn=1024):
    depth, d, n = ada_w.shape
    nb = cond.shape[0]
    return pl.pallas_call(
        _adaln_kernel,
        grid=(depth, n // tn),
        in_specs=[pl.BlockSpec((nb, d), lambda l, j: (0, 0)),
                  pl.BlockSpec((1, d, tn), lambda l, j: (l, 0, j)),
                  pl.BlockSpec((1, 1, tn), lambda l, j: (l, 0, j))],
        out_specs=pl.BlockSpec((1, nb, tn), lambda l, j: (l, 0, j)),
        out_shape=jax.ShapeDtypeStruct((depth, nb, n), F32),
        compiler_params=_cparams(2),
        name="adaln_mod",
    )(cond, ada_w, ada_b.reshape(depth, 1, n))


def _norm_mod_kernel(x_ref, g_ref, sc_ref, sh_ref, o_ref):
    x = x_ref[0]
    xn = x * lax.rsqrt(jnp.mean(x * x, axis=-1, keepdims=True) + NORM_EPS)
    o_ref[0] = ((xn * g_ref[...]) * (1.0 + sc_ref[0]) + sh_ref[0]).astype(o_ref.dtype)


def _norm_mod(x, g, sc, sh, tt=256):
    b, t, d = x.shape
    tt = _pick(t, tt)
    per_batch = sc.shape[0] > 1
    mod_map = (lambda i, j: (i, 0, 0)) if per_batch else (lambda i, j: (0, 0, 0))
    return pl.pallas_call(
        _norm_mod_kernel,
        grid=(b, t // tt),
        in_specs=[pl.BlockSpec((1, tt, d), lambda i, j: (i, j, 0)),
                  pl.BlockSpec((1, d), lambda i, j: (0, 0)),
                  pl.BlockSpec((1, 1, d), mod_map),
                  pl.BlockSpec((1, 1, d), mod_map)],
        out_specs=pl.BlockSpec((1, tt, d), lambda i, j: (i, j, 0)),
        out_shape=jax.ShapeDtypeStruct((b, t, d), BF16),
        compiler_params=_cparams(2),
        name="norm_mod",
    )(x, g.reshape(1, d), sc, sh)


def _final_norm_kernel(x_ref, g_ref, o_ref):
    x = x_ref[0]
    xn = x * lax.rsqrt(jnp.mean(x * x, axis=-1, keepdims=True) + NORM_EPS)
    o_ref[0] = xn * g_ref[...]


def _final_norm(x, g, tt=256):
    b, t, d = x.shape
    tt = _pick(t, tt)
    return pl.pallas_call(
        _final_norm_kernel,
        grid=(b, t // tt),
        in_specs=[pl.BlockSpec((1, tt, d), lambda i, j: (i, j, 0)),
                  pl.BlockSpec((1, d), lambda i, j: (0, 0))],
        out_specs=pl.BlockSpec((1, tt, d), lambda i, j: (i, j, 0)),
        out_shape=jax.ShapeDtypeStruct((b, t, d), F32),
        compiler_params=_cparams(2),
        name="final_norm",
    )(x, g.reshape(1, d))


def _mm_kernel(x_ref, w_ref, o_ref):
    o_ref[...] = _dot(x_ref[...], w_ref[...]).astype(o_ref.dtype)


def _pick(n, pref):
    t = min(n, pref)
    while n % t:
        t //= 2
    return t


def _matmul(x, w, out_dtype=F32, tm=1024, tn=512, name="matmul"):
    b, t, k = x.shape
    n = w.shape[1]
    m = b * t
    tm, tn = _pick(m, tm), _pick(n, tn)
    out = pl.pallas_call(
        _mm_kernel,
        grid=(m // tm, n // tn),
        in_specs=[pl.BlockSpec((tm, k), lambda i, j: (i, 0)),
                  pl.BlockSpec((k, tn), lambda i, j: (0, j))],
        out_specs=pl.BlockSpec((tm, tn), lambda i, j: (i, j)),
        out_shape=jax.ShapeDtypeStruct((m, n), out_dtype),
        compiler_params=_cparams(2),
        name=name,
    )(x.reshape(m, k), w)
    return out.reshape(b, t, n)


def _mm_res_kernel(x_ref, w_ref, r_ref, g_ref, o_ref):
    o_ref[...] = r_ref[...] + g_ref[0] * _dot(x_ref[...], w_ref[...])


def _matmul_residual(x, w, res, gate, tm=1024, tn=512, name="matmul_residual"):
    b, t, k = x.shape
    n = w.shape[1]
    m = b * t
    per_batch = gate.shape[0] > 1
    tm = _pick(t if per_batch else m, tm)
    tn = _pick(n, tn)
    gate_map = (lambda i, j: ((i * tm) // t, 0, j)) if per_batch else (lambda i, j: (0, 0, j))
    out = pl.pallas_call(
        _mm_res_kernel,
        grid=(m // tm, n // tn),
        in_specs=[pl.BlockSpec((tm, k), lambda i, j: (i, 0)),
                  pl.BlockSpec((k, tn), lambda i, j: (0, j)),
                  pl.BlockSpec((tm, tn), lambda i, j: (i, j)),
                  pl.BlockSpec((1, 1, tn), gate_map)],
        out_specs=pl.BlockSpec((tm, tn), lambda i, j: (i, j)),
        out_shape=jax.ShapeDtypeStruct((m, n), F32),
        compiler_params=_cparams(2),
        name=name,
    )(x.reshape(m, k), w, res.reshape(m, n), gate)
    return out.reshape(b, t, n)


def _mm_swiglu_kernel(x_ref, w1_ref, w3_ref, o_ref):
    x = x_ref[...]
    a = _dot(x, w1_ref[...])
    o_ref[...] = ((a * _sigmoid(a)) * _dot(x, w3_ref[...])).astype(o_ref.dtype)


def _matmul_swiglu(x, w1, w3, tm=1024, tn=512):
    b, t, k = x.shape
    n = w1.shape[1]
    m = b * t
    tm, tn = _pick(m, tm), _pick(n, tn)
    out = pl.pallas_call(
        _mm_swiglu_kernel,
        grid=(m // tm, n // tn),
        in_specs=[pl.BlockSpec((tm, k), lambda i, j: (i, 0)),
                  pl.BlockSpec((k, tn), lambda i, j: (0, j)),
                  pl.BlockSpec((k, tn), lambda i, j: (0, j))],
        out_specs=pl.BlockSpec((tm, tn), lambda i, j: (i, j)),
        out_shape=jax.ShapeDtypeStruct((m, n), BF16),
        compiler_params=_cparams(2),
        name="swiglu_up",
    )(x.reshape(m, k), w1, w3)
    return out.reshape(b, t, n)


def _mm_heads_kernel(x_ref, w_ref, o_ref, *, hb, hw):
    r = _dot(x_ref[0], w_ref[...])
    for i in range(hb):
        o_ref[0, i] = r[:, i * hw:(i + 1) * hw]


def _matmul_heads(x, w, hw, hb=2, tm=1024):
    b, t, k = x.shape
    nh = w.shape[1] // hw
    tm = _pick(t, tm)
    return pl.pallas_call(
        functools.partial(_mm_heads_kernel, hb=hb, hw=hw),
        grid=(b, t // tm, nh // hb),
        in_specs=[pl.BlockSpec((1, tm, k), lambda bi, i, j: (bi, i, 0)),
                  pl.BlockSpec((k, hb * hw), lambda bi, i, j: (0, j))],
        out_specs=pl.BlockSpec((1, hb, tm, hw), lambda bi, i, j: (bi, j, i, 0)),
        out_shape=jax.ShapeDtypeStruct((b, nh, t, hw), F32),
        compiler_params=_cparams(3),
        name="qkv_heads",
    )(x, w)


def _gla_kernel(*refs, seq, chunk, dk, has_s0, want_sf):
    q_ref, k_ref, v_ref, gg_ref, l1_ref, a2_ref, ab_ref, gn_ref = refs[:8]
    pos = 8
    s0_ref = None
    if has_s0:
        s0_ref = refs[pos]
        pos += 1
    o_ref = refs[pos]
    pos += 1
    sf_ref = None
    if want_sf:
        sf_ref = refs[pos]
        pos += 1
    acc_ref = refs[pos]

    n_chunks = seq // chunk
    dv = v_ref.shape[-1]
    scale = dk ** -0.5
    row = lax.broadcasted_iota(jnp.int32, (chunk, chunk), 0)
    col = lax.broadcasted_iota(jnp.int32, (chunk, chunk), 1)

    tri = [col <= row, col >= row]
    tri_bf = [jnp.where(t, 1.0, 0.0).astype(BF16) for t in tri]
    a2 = [a2_ref[d].astype(BF16) for d in range(2)]
    ab = [ab_ref[d] for d in range(2)]
    if has_s0:
        st0 = tuple(s0_ref[0, d, 0].T for d in range(2))
    else:
        st0 = tuple(jnp.zeros((dv, dk), F32) for d in range(2))
    unroll = 4 if n_chunks % 4 == 0 else (2 if n_chunks % 2 == 0 else 1)
    rb = min(seq, 256)

    def zero(i, carry):
        acc_ref[pl.ds(pl.multiple_of(i * rb, rb), rb), :] = jnp.zeros((rb, dv), F32)
        return carry

    lax.fori_loop(0, seq // rb, zero, 0)

    def body(i, carry):
        st = list(carry)
        chains = []
        for u in range(unroll):
            for d in range(2):
                n = i * unroll + u
                idx = n if d == 0 else n_chunks - 1 - n
                chains.append(dict(d=d, rows=pl.ds(pl.multiple_of(idx * chunk, chunk), chunk)))
        for ch in chains:
            d, rows = ch["d"], ch["rows"]
            l1 = l1_ref[0, rows, d * LANES:(d + 1) * LANES]
            ch["z"] = _dot(l1.astype(BF16), a2[d]) + ab[d]
        for ch in chains:
            ch["la"] = _log_sigmoid(ch["z"]) / GLA_GATE_NORM
        for ch in chains:
            ch["cum"] = _dot_exact_lhs(tri_bf[ch["d"]], ch["la"])
        for ch in chains:
            d, rows, cum = ch["d"], ch["rows"], ch["cum"]
            cl = cum[chunk - 1:chunk] if d == 0 else cum[0:1]
            k = k_ref[0, rows, :]
            ch["qe"] = ((q_ref[0, rows, :] * scale) * jnp.exp(cum)).astype(BF16)
            ch["ke"] = (k * jnp.exp(-cum)).astype(BF16)
            ch["kdec"] = (k * jnp.exp(cl - cum)).astype(BF16)
            ch["dec"] = jnp.exp(cl)
            ch["vb"] = v_ref[0, rows, :].astype(BF16)
        for ch in chains:
            ch["att"] = jnp.where(tri[ch["d"]], _dot(ch["qe"], ch["ke"], _NT), 0.0).astype(BF16)
        for ch in chains:
            ch["ds"] = _dot(ch["vb"], ch["kdec"], _TN)
            ch["o"] = _dot(ch["att"], ch["vb"])
        for ch in chains:
            d = ch["d"]
            ch["st_in"] = st[d].astype(BF16)
            st[d] = st[d] * ch["dec"] + ch["ds"]
        for ch in chains:
            rows = ch["rows"]
            acc_ref[rows, :] = acc_ref[rows, :] + (ch["o"] + _dot(ch["qe"], ch["st_in"], _NT))
        return tuple(st)

    st = lax.fori_loop(0, n_chunks // unroll, body, st0)
    if want_sf:
        for d in range(2):
            sf_ref[0, d, 0] = st[d].T

    gn = gn_ref[...]

    def epilogue(i, carry):
        rows = pl.ds(pl.multiple_of(i * rb, rb), rb)
        o = acc_ref[rows, :]
        on = o * lax.rsqrt(jnp.mean(o * o, axis=-1, keepdims=True) + NORM_EPS) * gn
        g = gg_ref[0, rows, :]
        o_ref[0, rows, :] = (on * (g * _sigmoid(g))).astype(o_ref.dtype)
        return carry

    lax.fori_loop(0, seq // rb, epilogue, 0)


def _gla(proj, a2p, ab, gnorm, s0, want_sf, heads, dk, dv, col_l1):
    b, t, _ = proj.shape
    has_s0 = s0 is not None
    kq = heads * dk // dk
    in_specs = [
        pl.BlockSpec((1, t, dk), lambda bi, h: (bi, 0, h)),
        pl.BlockSpec((1, t, dk), lambda bi, h: (bi, 0, heads + h)),
        pl.BlockSpec((1, t, dv), lambda bi, h: (bi, 0, (2 * heads * dk) // dv + h)),
        pl.BlockSpec((1, t, dv), lambda bi, h: (bi, 0, (2 * heads * dk) // dv + heads + h)),
        pl.BlockSpec((1, t, 2 * LANES), lambda bi, h: (bi, 0, col_l1 // (2 * LANES))),
        pl.BlockSpec((2, LANES, dk), lambda bi, h: (0, 0, h)),
        pl.BlockSpec((2, 1, dk), lambda bi, h: (0, 0, h)),
        pl.BlockSpec((1, dv), lambda bi, h: (0, 0)),
    ]
    args = [proj, proj, proj, proj, proj, a2p, ab, gnorm]
    if has_s0:
        in_specs.append(pl.BlockSpec((1, 2, 1, dk, dv), lambda bi, h: (bi, 0, h, 0, 0)))
        args.append(s0)
    out_specs = [pl.BlockSpec((1, t, dv), lambda bi, h: (bi, 0, h))]
    out_shape = [jax.ShapeDtypeStruct((b, t, heads * dv), BF16)]
    if want_sf:
        out_specs.append(pl.BlockSpec((1, 2, 1, dk, dv), lambda bi, h: (bi, 0, h, 0, 0)))
        out_shape.append(jax.ShapeDtypeStruct((b, 2, heads, dk, dv), F32))
    del kq
    res = pl.pallas_call(
        functools.partial(_gla_kernel, seq=t, chunk=GLA_CHUNK, dk=dk, has_s0=has_s0, want_sf=want_sf),
        grid=(b, heads),
        in_specs=in_specs,
        out_specs=out_specs,
        out_shape=out_shape,
        scratch_shapes=[pltpu.VMEM((t, dv), F32)],
        compiler_params=_cparams(2),
        name="gla_chunked",
    )(*args)
    return res if want_sf else (res[0], None)


def _rwkv_kernel(*refs, seq, chunk, want_sf):
    (r_ref, k_ref, v_ref, l1_ref, l1g_ref, w2_ref, a2_ref, g2_ref, w0_ref, a0_ref, mu_ref,
     kkw_ref, ka_ref, rk_ref, lng_ref, lnb_ref, s0_ref) = refs[:17]
    pos = 17
    o_ref = refs[pos]
    pos += 1
    sf_ref = None
    if want_sf:
        sf_ref = refs[pos]
        pos += 1
    rkv_s, rt_s, ys_s, g_s, ds_s = refs[pos:pos + 5]

    c = chunk
    c2 = 2 * chunk
    n_chunks = seq // c
    hn = RWKV_N

    lane = lax.broadcasted_iota(jnp.int32, (1, LANES), 1)
    head0 = lane < hn
    rowc = lax.broadcasted_iota(jnp.int32, (c, 1), 0)
    srow = lax.broadcasted_iota(jnp.int32, (c2, 1), 0)
    stack_mask = _div_pow2(srow, c) == _div_pow2(lane, hn)
    r2 = lax.broadcasted_iota(jnp.int32, (c2, c2), 0)
    q2 = lax.broadcasted_iota(jnp.int32, (c2, c2), 1)
    same_head = _div_pow2(r2, c) == _div_pow2(q2, c)
    rr = jnp.where(r2 >= c, r2 - c, r2)
    qq = jnp.where(q2 >= c, q2 - c, q2)
    eye2 = jnp.where(r2 == q2, 1.0, 0.0).astype(F32)
    lr = lax.broadcasted_iota(jnp.int32, (LANES, LANES), 0)
    lc = lax.broadcasted_iota(jnp.int32, (LANES, LANES), 1)
    diag_l = lr == lc
    block_l = _div_pow2(lr, hn) == _div_pow2(lc, hn)
    crow = lax.broadcasted_iota(jnp.int32, (c, c), 0)
    ccol = lax.broadcasted_iota(jnp.int32, (c, c), 1)

    def head_sum(x):
        s0 = jnp.sum(jnp.where(head0, x, 0.0), axis=-1, keepdims=True)
        s1 = jnp.sum(jnp.where(head0, 0.0, x), axis=-1, keepdims=True)
        return jnp.where(head0, s0, s1)

    def stack(x):
        return jnp.where(stack_mask, jnp.concatenate([x, x], axis=0), 0.0)

    def unstack(xs):
        return xs[:c] + xs[c:]

    mu = mu_ref[...]
    kkw = kkw_ref[...]
    ka = ka_ref[...]

    def shifted(ref, j, start, is_first, is_last):
        x = ref[0, pl.ds(start, c), :]
        pstart = pl.multiple_of(jnp.maximum(start - 8, 0), 8)
        nstart = pl.multiple_of(jnp.minimum(start + c, seq - 8), 8)
        prow = jnp.where(is_first, 0.0, ref[0, pl.ds(pstart, 8), :][7:8])
        nrow = jnp.where(is_last, 0.0, ref[0, pl.ds(nstart, 8), :][0:1])
        xp = jnp.where(rowc == 0, prow, pltpu.roll(x, 1, 0))
        xn = jnp.where(rowc == c - 1, nrow, pltpu.roll(x, c - 1, 0))
        return x + mu[0, j] * (xp - x) + mu[1, j] * (xn - x)

    tri_bf = [jnp.where(ccol <= crow, 1.0, 0.0).astype(BF16), jnp.where(ccol >= crow, 1.0, 0.0).astype(BF16)]
    n_levels = c.bit_length() - 1

    def level_mask(lvl, d):
        s = 1 << lvl
        siblings = jnp.logical_and(_div_pow2(rr, 2 * s) == _div_pow2(qq, 2 * s), same_head)
        r_late = (_div_pow2(rr, s) & 1) == 1
        q_late = (_div_pow2(qq, s) & 1) == 1
        if d == 0:
            return jnp.logical_and(siblings, jnp.logical_and(r_late, jnp.logical_not(q_late)))
        return jnp.logical_and(siblings, jnp.logical_and(jnp.logical_not(r_late), q_late))

    level_m = [[level_mask(lvl, d) for lvl in range(n_levels)] for d in range(2)]
    strict_m = [jnp.logical_and(qq < rr, same_head), jnp.logical_and(qq > rr, same_head)]
    incl_m = [jnp.logical_and(qq <= rr, same_head), jnp.logical_and(qq >= rr, same_head)]

    def phase_a(i, carry):
        chains = []
        for u in range(unroll):
            n = i * unroll + u
            start = pl.multiple_of(n * c, c)
            rows = pl.ds(start, c)
            is_first = n == 0
            is_last = n == n_chunks - 1
            r = shifted(r_ref, 0, start, is_first, is_last)
            k = shifted(k_ref, 1, start, is_first, is_last)
            v = shifted(v_ref, 2, start, is_first, is_last)
            rkv_s[0, rows, :] = r
            rkv_s[1, rows, :] = k
            rkv_s[2, rows, :] = v
            kk = k * kkw
            kk = kk * lax.rsqrt(head_sum(kk * kk) + 1e-12)
            vs_h, vs_l = _split2(stack(v))
            for d in range(2):
                chains.append(dict(n=n, rows=rows, d=d, r=r, k=k, kk=kk, vs_h=vs_h, vs_l=vs_l))

        for ch in chains:
            d, rows = ch["d"], ch["rows"]
            l1w = l1_ref[0, rows, d * LANES:(d + 1) * LANES]
            l1a = l1_ref[0, rows, (2 + d) * LANES:(3 + d) * LANES]
            ch["dlog"] = w0_ref[d] + _dot(jnp.tanh(l1w).astype(BF16), w2_ref[d])
            ch["alog"] = a0_ref[d] + _dot(l1a.astype(BF16), a2_ref[d])
        for ch in chains:
            dlog = ch["dlog"]
            sp = jnp.maximum(-dlog, 0.0) + jnp.log(1.0 + jnp.exp(-jnp.abs(dlog)))
            ch["lw"] = -jnp.exp(-sp - 0.5)
            a = _sigmoid(ch["alog"])
            ch["kd"] = ch["k"] * (1.0 + (a - 1.0) * ka)
            ch["bb"] = a * ch["kk"]
        for ch in chains:
            ch["lcum"] = _dot_exact_lhs(tri_bf[ch["d"]], ch["lw"])
        for ch in chains:
            d, lcum = ch["d"], ch["lcum"]
            ltot = lcum[c - 1:c] if d == 0 else lcum[0:1]
            e_nl = jnp.exp(-lcum)
            e_rem = jnp.exp(ltot - lcum)
            ch["e_tot"] = jnp.exp(ltot)
            ch["rs"] = stack(ch["r"] * jnp.exp(lcum))
            ch["ks_b"] = stack(ch["kk"] * jnp.exp(lcum - ch["lw"])).astype(BF16)
            kbs_b = stack(ch["kd"] * e_nl).astype(BF16)
            bbs_b = stack(ch["bb"] * e_nl).astype(BF16)
            ch["kts"] = _split2(stack(ch["kd"] * e_rem))
            ch["bts"] = _split2(stack(ch["bb"] * e_rem))
            ch["lhs"] = jnp.concatenate([ch["ks_b"], ch["rs"].astype(BF16)], axis=0)
            ch["rhs"] = jnp.concatenate([kbs_b, bbs_b], axis=0)
        for ch in chains:
            ch["aa"] = _dot(ch["lhs"], ch["rhs"], _NT)
        for ch in chains:
            d, aa = ch["d"], ch["aa"]
            ch["a_kk"] = jnp.where(strict_m[d], aa[:c2, :c2], 0.0).astype(BF16)
            ch["a_rk"] = jnp.where(incl_m[d], aa[c2:, :c2], 0.0).astype(BF16)
            ch["a_rb"] = jnp.where(incl_m[d], aa[c2:, c2:], 0.0).astype(BF16)
            ch["a_kb"] = aa[:c2, c2:]
            ch["tinv"] = eye2 - jnp.where(level_m[d][0], ch["a_kb"], 0.0)
        for ch in chains:
            ch["av"] = _dot(ch["a_kk"], ch["vs_h"]).astype(BF16)
        for ch in chains:
            (kh, kl), vh, vl = ch["kts"], ch["vs_h"], ch["vs_l"]
            ch["vk"] = _dot(vh, kh, _TN) + (_dot(vh, kl, _TN) + _dot(vl, kh, _TN))
        for lvl in range(1, n_levels):
            for ch in chains:
                ch["tb"] = ch["tinv"].astype(BF16)
                coupling = jnp.where(level_m[ch["d"]][lvl], ch["a_kb"], 0.0).astype(BF16)
                ch["e"] = _dot(coupling, ch["tb"]).astype(BF16)
            for ch in chains:
                ch["tinv"] = ch["tinv"] - _dot(ch["tb"], ch["e"])
        for ch in chains:
            wu = _dot(ch["tinv"].astype(BF16), jnp.concatenate([ch["ks_b"], ch["av"]], axis=1))
            ch["wu"] = _split2(wu)
        for ch in chains:
            ch["ru"] = _dot(ch["a_rb"], ch["wu"][0])
            ch["y0"] = _dot(ch["a_rk"], ch["vs_h"])
        for ch in chains:
            (wh, wl), (bh, bl) = ch["wu"], ch["bts"]
            ch["wb"] = _dot(wh, bh, _TN) + (_dot(wh, bl, _TN) + _dot(wl, bh, _TN))
        y0_sum = {}
        for ch in chains:
            d, n, rows = ch["d"], ch["n"], ch["rows"]
            rt = ch["rs"] - ch["ru"][:, :LANES]
            y0 = ch["y0"] - ch["ru"][:, LANES:]
            gmat = jnp.where(diag_l, ch["e_tot"], 0.0) - jnp.where(block_l, ch["wb"][:LANES], 0.0)
            dmat = jnp.where(block_l, ch["vk"] - ch["wb"][LANES:], 0.0)
            rt_s[d, rows, :] = unstack(rt).astype(BF16)
            g_h, g_l = _split2(gmat)
            g_s[0, d, n] = g_h
            g_s[1, d, n] = g_l
            ds_s[d, n] = dmat
            if d == 0:
                y0_sum = unstack(y0)
            else:
                ys_s[rows, :] = y0_sum + unstack(y0)
        return carry

    unroll = 4 if n_chunks % 4 == 0 else (2 if n_chunks % 2 == 0 else 1)
    lax.fori_loop(0, n_chunks // unroll, phase_a, 0)

    def phase_b(i, carry):
        sf, sb = carry
        out = []
        for d, st in ((0, sf), (1, sb)):
            idx = i if d == 0 else n_chunks - 1 - i
            rows = pl.ds(pl.multiple_of(idx * c, c), c)
            st_h, st_l = _split2(st)
            ys_s[rows, :] = ys_s[rows, :] + _dot(rt_s[d, rows, :], st_h, _NT)
            g_h = g_s[0, d, idx]
            out.append(_dot(st_h, g_h) + (_dot(st_h, g_s[1, d, idx]) + _dot(st_l, g_h)) + ds_s[d, idx])
        return tuple(out)

    sf, sb = lax.fori_loop(0, n_chunks, phase_b, (s0_ref[0, 0, 0], s0_ref[0, 1, 0]))
    if want_sf:
        sf_ref[0, 0, 0] = sf
        sf_ref[0, 1, 0] = sb

    rb = min(seq, 256)
    lng = lng_ref[...]
    lnb = lnb_ref[...]
    rkw = rk_ref[...]
    g2 = g2_ref[...]

    def epilogue(i, carry):
        rows = pl.ds(pl.multiple_of(i * rb, rb), rb)
        y = ys_s[rows, :]
        mean = head_sum(y) * (1.0 / hn)
        yc = y - mean
        var = head_sum(yc * yc) * (1.0 / hn)
        yn = yc * lax.rsqrt(var + RWKV_LN_EPS) * lng + lnb
        r = rkv_s[0, rows, :]
        k = rkv_s[1, rows, :]
        v = rkv_s[2, rows, :]
        bonus = head_sum(r * k * rkw) * v
        gate = _dot(_sigmoid(l1g_ref[0, rows, :]).astype(BF16), g2)
        o_ref[0, rows, :] = ((yn + bonus) * gate).astype(o_ref.dtype)
        return carry

    lax.fori_loop(0, seq // rb, epilogue, 0)


def _rwkv(proj, w2p, a2p, g2, w0, a0, mu, kkw, ka, rkw, lng, lnb, s0_bd, want_sf, col_rkv, col_l1, col_g):
    b, t, _ = proj.shape
    width = kkw.shape[-1]
    pairs = width // LANES
    c = RWKV_CHUNK
    n_chunks = t // c
    cb = col_rkv // LANES
    vec = lambda: pl.BlockSpec((1, LANES), lambda bi, h: (0, h))
    in_specs = [
        pl.BlockSpec((1, t, LANES), lambda bi, h: (bi, 0, cb + h)),
        pl.BlockSpec((1, t, LANES), lambda bi, h: (bi, 0, cb + pairs + h)),
        pl.BlockSpec((1, t, LANES), lambda bi, h: (bi, 0, cb + 2 * pairs + h)),
        pl.BlockSpec((1, t, 4 * LANES), lambda bi, h: (bi, 0, col_l1 // (4 * LANES))),
        pl.BlockSpec((1, t, 2 * LANES), lambda bi, h: (bi, 0, col_g // (2 * LANES))),
        pl.BlockSpec((2, LANES, LANES), lambda bi, h: (0, 0, h)),
        pl.BlockSpec((2, LANES, LANES), lambda bi, h: (0, 0, h)),
        pl.BlockSpec((2 * LANES, LANES), lambda bi, h: (0, h)),
        pl.BlockSpec((2, 1, LANES), lambda bi, h: (0, 0, h)),
        pl.BlockSpec((2, 1, LANES), lambda bi, h: (0, 0, h)),
        pl.BlockSpec((2, 3, 1, LANES), lambda bi, h: (0, 0, 0, h)),
        vec(), vec(), vec(), vec(), vec(),
        pl.BlockSpec((1, 2, 1, LANES, LANES), lambda bi, h: (bi, 0, h, 0, 0)),
    ]
    out_specs = [pl.BlockSpec((1, t, LANES), lambda bi, h: (bi, 0, h))]
    out_shape = [jax.ShapeDtypeStruct((b, t, width), BF16)]
    if want_sf:
        out_specs.append(pl.BlockSpec((1, 2, 1, LANES, LANES), lambda bi, h: (bi, 0, h, 0, 0)))
        out_shape.append(jax.ShapeDtypeStruct((b, 2, pairs, LANES, LANES), F32))
    res = pl.pallas_call(
        functools.partial(_rwkv_kernel, seq=t, chunk=c, want_sf=want_sf),
        grid=(b, pairs),
        in_specs=in_specs,
        out_specs=out_specs,
        out_shape=out_shape,
        scratch_shapes=[pltpu.VMEM((3, t, LANES), F32),
                        pltpu.VMEM((2, t, LANES), BF16),
                        pltpu.VMEM((t, LANES), F32),
                        pltpu.VMEM((2, 2, n_chunks, LANES, LANES), BF16),
                        pltpu.VMEM((2, n_chunks, LANES, LANES), F32)],
        compiler_params=_cparams(2),
        name="rwkv7_chunked",
    )(proj, proj, proj, proj, proj, w2p, a2p, g2, w0, a0, mu, kkw, ka, rkw, lng, lnb, s0_bd)
    return res if want_sf else (res[0], None)


def _rope(x, cos, sin, first_half):
    part = jnp.where(first_half, pltpu.roll(x, x.shape[-1] - 32, 1), pltpu.roll(x, 32, 1))
    return x * cos + part * sin


def _diff_attn_kernel(*refs, rope, has_cache, lam_init, tq):
    q_ref, k_ref, v_ref, lq_ref, lk_ref, sub_ref = refs[:6]
    pos = 6
    if rope:
        cos_ref, sin_ref = refs[pos:pos + 2]
        pos += 2
    if has_cache:
        ck_ref, cv_ref = refs[pos:pos + 2]
        pos += 2
    o_ref = refs[pos]
    ks_ref, vs_ref = refs[pos + 1:pos + 3]

    t = k_ref.shape[2]
    w = k_ref.shape[3]
    dh = DIFF_DH
    i = pl.program_id(2)
    lane = lax.broadcasted_iota(jnp.int32, (1, w), 1)
    first_half = (lane & 63) < 32

    @pl.when(i == 0)
    def _():
        kb = min(t, 512)

        def fill(j, carry):
            rows = pl.ds(pl.multiple_of(j * kb, kb), kb)
            kx = k_ref[0, 0, rows, :]
            if rope:
                kx = _rope(kx, cos_ref[rows, :], sin_ref[rows, :], first_half)
            ks_ref[rows, :] = kx.astype(BF16)
            vs_ref[rows, :] = v_ref[0, 0, rows, :].astype(BF16)
            return carry

        lax.fori_loop(0, t // kb, fill, 0)
        if has_cache:
            ks_ref[t:, :] = ck_ref[0, 0, 0].astype(BF16)
            vs_ref[t:, :] = cv_ref[0, 0, 0].astype(BF16)

    q = q_ref[0, 0]
    if rope:
        qrows = pl.ds(pl.multiple_of(i * tq, tq), tq)
        q = _rope(q, cos_ref[qrows, :], sin_ref[qrows, :], first_half)
    qb = q.astype(BF16)
    lq = lq_ref[...]
    lk = lk_ref[...]
    lam = (jnp.exp(jnp.sum(lq[0:1] * lk[0:1], axis=-1, keepdims=True))
           - jnp.exp(jnp.sum(lq[1:2] * lk[1:2], axis=-1, keepdims=True)) + lam_init)
    kall = ks_ref[...]
    vall = vs_ref[...]
    exp2_scale = dh ** -0.5 * math.log2(math.e)
    sub = min(tq, 128)
    scores = [[_dot(qb[r:r + sub, m * dh:(m + 1) * dh], kall[:, m * dh:(m + 1) * dh], _NT) for m in range(2)]
              for r in range(0, tq, sub)]
    rows_out = []
    for pair in scores:
        outs = []
        for s in pair:
            e = jnp.exp2((s - jnp.max(s, axis=-1, keepdims=True)) * exp2_scale)
            inv = 1.0 / jnp.sum(e, axis=-1, keepdims=True)
            outs.append(_dot(e.astype(BF16), vall) * inv)
        rows_out.append(outs[0] - lam * outs[1])
    o = jnp.concatenate(rows_out, axis=0) if len(rows_out) > 1 else rows_out[0]
    on = o * lax.rsqrt(jnp.mean(o * o, axis=-1, keepdims=True) + SUBLN_EPS) * sub_ref[...]
    o_ref[0] = (on * (1.0 - lam_init)).astype(o_ref.dtype)


def _diff_attention(qkv, heads, lq, lk, subln, lam_init, rope_tabs, cache_k, cache_v, tq=256):
    b, _, t, w = qkv.shape
    tq = _pick(t, tq)
    rope = rope_tabs is not None
    has_cache = cache_k is not None
    tk = t + (cache_k.shape[3] if has_cache else 0)
    in_specs = [
        pl.BlockSpec((1, 1, tq, w), lambda bi, h, i: (bi, h, i, 0)),
        pl.BlockSpec((1, 1, t, w), lambda bi, h, i: (bi, heads + h, 0, 0)),
        pl.BlockSpec((1, 1, t, w), lambda bi, h, i: (bi, 2 * heads + h, 0, 0)),
        pl.BlockSpec((2, DIFF_DH), lambda bi, h, i: (0, 0)),
        pl.BlockSpec((2, DIFF_DH), lambda bi, h, i: (0, 0)),
        pl.BlockSpec((1, w), lambda bi, h, i: (0, 0)),
    ]
    args = [qkv, qkv, qkv, lq, lk, subln.reshape(1, w)]
    if rope:
        in_specs += [pl.BlockSpec((t, w), lambda bi, h, i: (0, 0))] * 2
        args += list(rope_tabs)
    if has_cache:
        pc = cache_k.shape[3]
        in_specs += [pl.BlockSpec((1, 1, 1, pc, w), lambda bi, h, i: (bi, 0, h, 0, 0))] * 2
        args += [cache_k, cache_v]
    return pl.pallas_call(
        functools.partial(_diff_attn_kernel, rope=rope, has_cache=has_cache, lam_init=lam_init, tq=tq),
        grid=(b, heads, t // tq),
        in_specs=in_specs,
        out_specs=pl.BlockSpec((1, tq, w), lambda bi, h, i: (bi, i, h)),
        out_shape=jax.ShapeDtypeStruct((b, t, heads * w), BF16),
        scratch_shapes=[pltpu.VMEM((tk, w), BF16), pltpu.VMEM((tk, w), BF16)],
        compiler_params=_cparams(3),
        name="diff_attention",
    )(*args)


def _rope_tables(t, w):
    nf = DIFF_DH // 4
    rows = t // GRID_W
    row = jnp.repeat(jnp.arange(rows, dtype=F32), GRID_W)
    col = jnp.tile(jnp.arange(GRID_W, dtype=F32), rows)
    inv = jnp.power(ROPE_BASE, -jnp.arange(nf, dtype=F32) / nf)
    ang_r = row[:, None] * inv
    ang_c = col[:, None] * inv
    cos = jnp.concatenate([jnp.cos(ang_r), jnp.cos(ang_r), jnp.cos(ang_c), jnp.cos(ang_c)], axis=-1)
    sin = jnp.concatenate([-jnp.sin(ang_r), jnp.sin(ang_r), -jnp.sin(ang_c), jnp.sin(ang_c)], axis=-1)
    reps = w // DIFF_DH
    return jnp.tile(cos, (1, reps)), jnp.tile(sin, (1, reps))


def _pad_rows(w, rows):
    return jnp.pad(w, ((0, 0),) * (w.ndim - 2) + ((0, rows - w.shape[-2]), (0, 0)))


def _pad_cols(w, cols):
    return jnp.pad(w, ((0, 0),) * (w.ndim - 1) + ((0, cols - w.shape[-1]),))


def kernel(x_prompt, x_sample, state_gla, state_rwkv, cache_k, cache_v, c, c_ctx, ada_w, ada_b, norm_mix, norm_ffn, norm_final, ffn_w1, ffn_w3, ffn_w2, even_w_in, even_w_out, gla_a1, gla_a2, gla_ab, gla_norm, rw_mu, rw_w0, rw_w1, rw_w2, rw_a0, rw_a1, rw_a2, rw_g1, rw_g2, rw_kk, rw_ka, rw_rk, rw_ln_g, rw_ln_b, odd_w_in, odd_w_out, diff_lq, diff_lk, diff_subln):
    depth, d, _ = ada_w.shape
    dec_b = x_sample.shape[0]
    gla_heads, gla_dk, gla_dv = state_gla.shape[3:]
    rw_heads = state_rwkv.shape[3]
    rw_w = rw_heads * RWKV_N
    pairs = rw_w // LANES
    diff_heads, _, diff_w = cache_k.shape[2:]
    gla_qk = gla_heads * gla_dk
    gla_vw = gla_heads * gla_dv
    even_in = even_w_in.shape[-1]

    nb = -(-(1 + dec_b) // 8) * 8
    cond = jnp.zeros((nb, d), F32).at[0].set(c_ctx).at[1:1 + dec_b].set(c)
    mod = _adaln(cond, ada_w, ada_b)

    def mods(layer, prompt):
        m = mod[layer, 0:1] if prompt else mod[layer, 1:1 + dec_b]
        return [m[:, None, k * d:(k + 1) * d] for k in range(6)]

    def even_weights(e):
        lora1 = jnp.concatenate(
            [_pad_cols(rw_w1[e, 0], LANES), _pad_cols(rw_w1[e, 1], LANES),
             _pad_cols(rw_a1[e, 0], LANES), _pad_cols(rw_a1[e, 1], LANES),
             rw_g1[e],
             _pad_cols(gla_a1[e, 0], LANES), _pad_cols(gla_a1[e, 1], LANES)], axis=-1)
        w_cat = jnp.concatenate([even_w_in[e], lora1], axis=-1).astype(BF16)
        return dict(
            w_cat=w_cat,
            w_out=even_w_out[e].astype(BF16),
            gla_a2=_pad_rows(gla_a2[e], LANES),
            gla_ab=gla_ab[e].reshape(2, 1, gla_qk),
            gla_norm=gla_norm[e].reshape(1, gla_dv),
            w2=_pad_rows(rw_w2[e], LANES).astype(BF16),
            a2=_pad_rows(rw_a2[e], LANES).astype(BF16),
            g2=rw_g2[e].astype(BF16),
            w0=rw_w0[e].reshape(2, 1, rw_w),
            a0=rw_a0[e].reshape(2, 1, rw_w),
            mu=rw_mu[e].reshape(2, 3, 1, rw_w),
            kk=rw_kk[e].reshape(1, rw_w), ka=rw_ka[e].reshape(1, rw_w), rk=rw_rk[e].reshape(1, rw_w),
            lng=rw_ln_g[e].reshape(1, rw_w), lnb=rw_ln_b[e].reshape(1, rw_w),
        )

    col_rkv = 2 * gla_qk + 2 * gla_vw
    col_l1 = even_in
    col_g = even_in + 4 * LANES
    col_gl = even_in + 6 * LANES

    def block_diag_state(s):
        bsz = s.shape[0]
        s = s.reshape(bsz, 2, pairs, 2, RWKV_N, RWKV_N)
        z = jnp.zeros_like(s[:, :, :, 0])
        top = jnp.concatenate([s[:, :, :, 0], z], axis=-1)
        bot = jnp.concatenate([z, s[:, :, :, 1]], axis=-1)
        return jnp.concatenate([top, bot], axis=-2)

    def diag_blocks(sbd):
        bsz = sbd.shape[0]
        h0 = sbd[:, :, :, :RWKV_N, :RWKV_N]
        h1 = sbd[:, :, :, RWKV_N:, RWKV_N:]
        return jnp.stack([h0, h1], axis=3).reshape(bsz, 2, rw_heads, RWKV_N, RWKV_N)

    def even_mixer(h, ew, s_gla, s_rwkv, want_states):
        proj = _matmul(h, ew["w_cat"], name="even_in_proj")
        o_gla, sg = _gla(proj, ew["gla_a2"], ew["gla_ab"], ew["gla_norm"], s_gla, want_states,
                         gla_heads, gla_dk, gla_dv, col_gl)
        if s_rwkv is None:
            s_bd = jnp.zeros((h.shape[0], 2, pairs, LANES, LANES), F32)
        else:
            s_bd = block_diag_state(s_rwkv)
        y_rw, sr = _rwkv(proj, ew["w2"], ew["a2"], ew["g2"], ew["w0"], ew["a0"], ew["mu"], ew["kk"], ew["ka"],
                         ew["rk"], ew["lng"], ew["lnb"], s_bd, want_states, col_rkv, col_l1, col_g)
        mix = jnp.concatenate([o_gla, y_rw], axis=-1)
        return mix, ew["w_out"], sg, (diag_blocks(sr) if want_states else None)

    def run_group(x, prompt):
        outs = {}
        for layer in range(depth):
            sh1, sc1, g1, sh2, sc2, g2 = mods(layer, prompt)
            h = _norm_mod(x, norm_mix[layer], sc1, sh1)
            if layer % 2 == 0:
                e = layer // 2
                ew = even_weights(e)
                if prompt:
                    mix, w_out, sg, sr = even_mixer(h, ew, None, None, True)
                    outs.setdefault("gla", []).append(sg)
                    outs.setdefault("rwkv", []).append(sr)
                else:
                    mix, w_out, _, _ = even_mixer(h, ew, state_gla[:, e], state_rwkv[:, e], False)
            else:
                o = layer // 2
                lam_init = 0.8 - 0.6 * math.exp(-0.3 * layer)
                qkv = _matmul_heads(h, odd_w_in[o].astype(BF16), diff_w)
                if prompt:
                    outs.setdefault("k", []).append(qkv[:, diff_heads:2 * diff_heads])
                    outs.setdefault("v", []).append(qkv[:, 2 * diff_heads:])
                    mix = _diff_attention(qkv, diff_heads, diff_lq[o], diff_lk[o], diff_subln[o], lam_init,
                                          None, None, None)
                else:
                    mix = _diff_attention(qkv, diff_heads, diff_lq[o], diff_lk[o], diff_subln[o], lam_init,
                                          _rope_tables(x.shape[1], diff_w), cache_k[:, o:o + 1], cache_v[:, o:o + 1])
                w_out = odd_w_out[o].astype(BF16)
            x = _matmul_residual(mix, w_out, x, g1, name="mix_out_proj")
            h = _norm_mod(x, norm_ffn[layer], sc2, sh2)
            u = _matmul_swiglu(h, ffn_w1[layer].astype(BF16), ffn_w3[layer].astype(BF16))
            x = _matmul_residual(u, ffn_w2[layer].astype(BF16), x, g2, name="ffn_down_proj")
        return _final_norm(x, norm_final), outs

    y_prompt, po = run_group(x_prompt, True)
    y_sample, _ = run_group(x_sample, False)
    return (y_prompt, y_sample,
            jnp.stack(po["gla"], axis=1), jnp.stack(po["rwkv"], axis=1),
            jnp.stack(po["k"], axis=1), jnp.stack(po["v"], axis=1))
```

```python
import functools
import math

import jax
import jax.numpy as jnp
from jax import lax
from jax.experimental import pallas as pl
from jax.experimental.pallas import tpu as pltpu

F32 = jnp.float32
BF16 = jnp.bfloat16

V7X_VMEM_LIMIT_BYTES = 56 * 1024 * 1024
LANES = 128

NORM_EPS = 1e-6
GLA_GATE_NORM = 16.0
GLA_CHUNK = 64
RWKV_CHUNK = 64
RWKV_N = 64
RWKV_LN_EPS = 64e-5
DIFF_DH = 128
GRID_W = 64
ROPE_BASE = 10000.0
SUBLN_EPS = 1e-5


def _cparams(n_axes):
    return pltpu.CompilerParams(dimension_semantics=("arbitrary",) * n_axes,
                                vmem_limit_bytes=V7X_VMEM_LIMIT_BYTES)


def _dot(a, b, dims=(((1,), (0,)), ((), ()))):
    return lax.dot_general(a, b, dims, preferred_element_type=F32)


_NN = (((1,), (0,)), ((), ()))
_NT = (((1,), (1,)), ((), ()))
_TN = (((0,), (0,)), ((), ()))


def _split3(x):
    h = x.astype(BF16)
    r = x - h.astype(F32)
    m = r.astype(BF16)
    l = (r - m.astype(F32)).astype(BF16)
    return h, m, l


def _split2(x):
    h = x.astype(BF16)
    m = (x - h.astype(F32)).astype(BF16)
    return h, m


def _dot_hp(a, b, dims=_NN):
    ah, al = _split2(a)
    bh, bl = _split2(b)
    return _dot(ah, bh, dims) + (_dot(ah, bl, dims) + _dot(al, bh, dims))


def _dot_exact_lhs(a_bf, b, dims=_NN):
    h, m, l = _split3(b)
    return _dot(a_bf, h, dims) + (_dot(a_bf, m, dims) + _dot(a_bf, l, dims))


def _div_pow2(x, n):
    assert n & (n - 1) == 0
    return lax.shift_right_logical(x, n.bit_length() - 1)


def _sigmoid(x):
    return 1.0 / (1.0 + jnp.exp(-x))


def _log_sigmoid(z):
    return jnp.minimum(z, 0.0) - jnp.log(1.0 + jnp.exp(-jnp.abs(z)))


def _adaln_kernel(c_ref, w_ref, b_ref, o_ref):
    cs = c_ref[...]
    s = cs * _sigmoid(cs)
    o_ref[0] = _dot(s.astype(BF16), w_ref[0].astype(BF16)) + b_ref[0]


def _adaln(cond, ada_w, ada_b, tn=1024):
    depth, d, n = ada_w.shape
    nb = cond.shape[0]
    return pl.pallas_call(
        _adaln_kernel,
        grid=(depth, n // tn),
        in_specs=[pl.BlockSpec((nb, d), lambda l, j: (0, 0)),
                  pl.BlockSpec((1, d, tn), lambda l, j: (l, 0, j)),
                  pl.BlockSpec((1, 1, tn), lambda l, j: (l, 0, j))],
        out_specs=pl.BlockSpec((1, nb, tn), lambda l, j: (l, 0, j)),
        out_shape=jax.ShapeDtypeStruct((depth, nb, n), F32),
        compiler_params=_cparams(2),
        name="adaln_mod",
    )(cond, ada_w, ada_b.reshape(depth, 1, n))


def _norm_mod_kernel(x_ref, g_ref, sc_ref, sh_ref, o_ref):
    x = x_ref[0]
    xn = x * lax.rsqrt(jnp.mean(x * x, axis=-1, keepdims=True) + NORM_EPS)
    o_ref[0] = ((xn * g_ref[...]) * (1.0 + sc_ref[0]) + sh_ref[0]).astype(o_ref.dtype)


def _norm_mod(x, g, sc, sh, tt=1024):
    b, t, d = x.shape
    tt = _pick(t, tt)
    per_batch = sc.shape[0] > 1
    mod_map = (lambda i, j: (i, 0, 0)) if per_batch else (lambda i, j: (0, 0, 0))
    return pl.pallas_call(
        _norm_mod_kernel,
        grid=(b, t // tt),
        in_specs=[pl.BlockSpec((1, tt, d), lambda i, j: (i, j, 0)),
                  pl.BlockSpec((1, d), lambda i, j: (0, 0)),
                  pl.BlockSpec((1, 1, d), mod_map),
                  pl.BlockSpec((1, 1, d), mod_map)],
        out_specs=pl.BlockSpec((1, tt, d), lambda i, j: (i, j, 0)),
        out_shape=jax.ShapeDtypeStruct((b, t, d), BF16),
        compiler_params=_cparams(2),
        name="norm_mod",
    )(x, g.reshape(1, d), sc, sh)


def _final_norm_kernel(x_ref, g_ref, o_ref):
    x = x_ref[0]
    xn = x * lax.rsqrt(jnp.mean(x * x, axis=-1, keepdims=True) + NORM_EPS)
    o_ref[0] = xn * g_ref[...]


def _final_norm(x, g, tt=1024):
    b, t, d = x.shape
    tt = _pick(t, tt)
    return pl.pallas_call(
        _final_norm_kernel,
        grid=(b, t // tt),
        in_specs=[pl.BlockSpec((1, tt, d), lambda i, j: (i, j, 0)),
                  pl.BlockSpec((1, d), lambda i, j: (0, 0))],
        out_specs=pl.BlockSpec((1, tt, d), lambda i, j: (i, j, 0)),
        out_shape=jax.ShapeDtypeStruct((b, t, d), F32),
        compiler_params=_cparams(2),
        name="final_norm",
    )(x, g.reshape(1, d))


def _mm_kernel(x_ref, w_ref, o_ref):
    o_ref[...] = _dot(x_ref[...], w_ref[...]).astype(o_ref.dtype)


def _pick(n, pref):
    t = min(n, pref)
    while n % t:
        t //= 2
    return t


def _matmul(x, w, out_dtype=F32, tm=1024, tn=512, name="matmul"):
    b, t, k = x.shape
    n = w.shape[1]
    m = b * t
    tm, tn = _pick(m, tm), _pick(n, tn)
    out = pl.pallas_call(
        _mm_kernel,
        grid=(m // tm, n // tn),
        in_specs=[pl.BlockSpec((tm, k), lambda i, j: (i, 0)),
                  pl.BlockSpec((k, tn), lambda i, j: (0, j))],
        out_specs=pl.BlockSpec((tm, tn), lambda i, j: (i, j)),
        out_shape=jax.ShapeDtypeStruct((m, n), out_dtype),
        compiler_params=_cparams(2),
        name=name,
    )(x.reshape(m, k), w)
    return out.reshape(b, t, n)


def _mm_res_kernel(x_ref, w_ref, r_ref, g_ref, o_ref):
    o_ref[...] = r_ref[...] + g_ref[0] * _dot(x_ref[...], w_ref[...].astype(BF16))


def _matmul_residual(x, w, res, gate, tm=1024, tn=512, name="matmul_residual"):
    b, t, k = x.shape
    n = w.shape[1]
    m = b * t
    per_batch = gate.shape[0] > 1
    tm = _pick(t if per_batch else m, tm)
    tn = _pick(n, tn)
    gate_map = (lambda i, j: ((i * tm) // t, 0, j)) if per_batch else (lambda i, j: (0, 0, j))
    out = pl.pallas_call(
        _mm_res_kernel,
        grid=(m // tm, n // tn),
        in_specs=[pl.BlockSpec((tm, k), lambda i, j: (i, 0)),
                  pl.BlockSpec((k, tn), lambda i, j: (0, j)),
                  pl.BlockSpec((tm, tn), lambda i, j: (i, j)),
                  pl.BlockSpec((1, 1, tn), gate_map)],
        out_specs=pl.BlockSpec((tm, tn), lambda i, j: (i, j)),
        out_shape=jax.ShapeDtypeStruct((m, n), F32),
        compiler_params=_cparams(2),
        name=name,
    )(x.reshape(m, k), w, res.reshape(m, n), gate)
    return out.reshape(b, t, n)


def _mm_swiglu_kernel(x_ref, w1_ref, w3_ref, o_ref):
    x = x_ref[...]
    a = _dot(x, w1_ref[...].astype(BF16))
    o_ref[...] = ((a * _sigmoid(a)) * _dot(x, w3_ref[...].astype(BF16))).astype(o_ref.dtype)


def _matmul_swiglu(x, w1, w3, tm=1024, tn=512):
    b, t, k = x.shape
    n = w1.shape[1]
    m = b * t
    tm, tn = _pick(m, tm), _pick(n, tn)
    out = pl.pallas_call(
        _mm_swiglu_kernel,
        grid=(m // tm, n // tn),
        in_specs=[pl.BlockSpec((tm, k), lambda i, j: (i, 0)),
                  pl.BlockSpec((k, tn), lambda i, j: (0, j)),
                  pl.BlockSpec((k, tn), lambda i, j: (0, j))],
        out_specs=pl.BlockSpec((tm, tn), lambda i, j: (i, j)),
        out_shape=jax.ShapeDtypeStruct((m, n), BF16),
        compiler_params=_cparams(2),
        name="swiglu_up",
    )(x.reshape(m, k), w1, w3)
    return out.reshape(b, t, n)


def _mm_heads_kernel(x_ref, w_ref, o_ref, *, hb, hw):
    r = _dot(x_ref[0], w_ref[...].astype(BF16))
    for i in range(hb):
        o_ref[0, i] = r[:, i * hw:(i + 1) * hw]


def _matmul_heads(x, w, hw, hb=2, tm=1024):
    b, t, k = x.shape
    nh = w.shape[1] // hw
    tm = _pick(t, tm)
    return pl.pallas_call(
        functools.partial(_mm_heads_kernel, hb=hb, hw=hw),
        grid=(b, t // tm, nh // hb),
        in_specs=[pl.BlockSpec((1, tm, k), lambda bi, i, j: (bi, i, 0)),
                  pl.BlockSpec((k, hb * hw), lambda bi, i, j: (0, j))],
        out_specs=pl.BlockSpec((1, hb, tm, hw), lambda bi, i, j: (bi, j, i, 0)),
        out_shape=jax.ShapeDtypeStruct((b, nh, t, hw), F32),
        compiler_params=_cparams(3),
        name="qkv_heads",
    )(x, w)


def _gla_kernel(*refs, seq, chunk, dk, has_s0, want_sf):
    q_ref, k_ref, v_ref, gg_ref, l1_ref, a2_ref, ab_ref, gn_ref = refs[:8]
    pos = 8
    s0_ref = None
    if has_s0:
        s0_ref = refs[pos]
        pos += 1
    o_ref = refs[pos]
    pos += 1
    sf_ref = None
    if want_sf:
        sf_ref = refs[pos]
        pos += 1
    acc_ref = refs[pos]

    n_chunks = seq // chunk
    dv = v_ref.shape[-1]
    scale = dk ** -0.5
    row = lax.broadcasted_iota(jnp.int32, (chunk, chunk), 0)
    col = lax.broadcasted_iota(jnp.int32, (chunk, chunk), 1)

    tri = [col <= row, col >= row]
    tri_bf = [jnp.where(t, 1.0, 0.0).astype(BF16) for t in tri]
    a2 = [a2_ref[d].astype(BF16) for d in range(2)]
    ab = [ab_ref[d] for d in range(2)]
    if has_s0:
        st0 = tuple(s0_ref[0, d, 0].T for d in range(2))
    else:
        st0 = tuple(jnp.zeros((dv, dk), F32) for d in range(2))
    unroll = 4 if n_chunks % 4 == 0 else (2 if n_chunks % 2 == 0 else 1)
    rb = min(seq, 256)

    def zero(i, carry):
        acc_ref[pl.ds(pl.multiple_of(i * rb, rb), rb), :] = jnp.zeros((rb, dv), F32)
        return carry

    lax.fori_loop(0, seq // rb, zero, 0)

    def body(i, carry):
        st = list(carry)
        chains = []
        for u in range(unroll):
            for d in range(2):
                n = i * unroll + u
                idx = n if d == 0 else n_chunks - 1 - n
                chains.append(dict(d=d, rows=pl.ds(pl.multiple_of(idx * chunk, chunk), chunk)))
        for ch in chains:
            d, rows = ch["d"], ch["rows"]
            l1 = l1_ref[0, rows, d * LANES:(d + 1) * LANES]
            ch["z"] = _dot(l1.astype(BF16), a2[d]) + ab[d]
        for ch in chains:
            ch["la"] = _log_sigmoid(ch["z"]) / GLA_GATE_NORM
        for ch in chains:
            ch["cum"] = _dot_exact_lhs(tri_bf[ch["d"]], ch["la"])
        for ch in chains:
            d, rows, cum = ch["d"], ch["rows"], ch["cum"]
            cl = cum[chunk - 1:chunk] if d == 0 else cum[0:1]
            k = k_ref[0, rows, :]
            ch["qe"] = ((q_ref[0, rows, :] * scale) * jnp.exp(cum)).astype(BF16)
            ch["ke"] = (k * jnp.exp(-cum)).astype(BF16)
            ch["kdec"] = (k * jnp.exp(cl - cum)).astype(BF16)
            ch["dec"] = jnp.exp(cl)
            ch["vb"] = v_ref[0, rows, :].astype(BF16)
        for ch in chains:
            ch["att"] = jnp.where(tri[ch["d"]], _dot(ch["qe"], ch["ke"], _NT), 0.0).astype(BF16)
        for ch in chains:
            ch["ds"] = _dot(ch["vb"], ch["kdec"], _TN)
            ch["o"] = _dot(ch["att"], ch["vb"])
        for ch in chains:
            d = ch["d"]
            ch["st_in"] = st[d].astype(BF16)
            st[d] = st[d] * ch["dec"] + ch["ds"]
        for ch in chains:
            rows = ch["rows"]
            acc_ref[rows, :] = acc_ref[rows, :] + (ch["o"] + _dot(ch["qe"], ch["st_in"], _NT))
        return tuple(st)

    st = lax.fori_loop(0, n_chunks // unroll, body, st0)
    if want_sf:
        for d in range(2):
            sf_ref[0, d, 0] = st[d].T

    gn = gn_ref[...]

    def epilogue(i, carry):
        rows = pl.ds(pl.multiple_of(i * rb, rb), rb)
        o = acc_ref[rows, :]
        on = o * lax.rsqrt(jnp.mean(o * o, axis=-1, keepdims=True) + NORM_EPS) * gn
        g = gg_ref[0, rows, :]
        o_ref[0, rows, :] = (on * (g * _sigmoid(g))).astype(o_ref.dtype)
        return carry

    lax.fori_loop(0, seq // rb, epilogue, 0)


def _gla(proj, a2p, ab, gnorm, s0, want_sf, heads, dk, dv, col_l1):
    b, t, _ = proj.shape
    has_s0 = s0 is not None
    kq = heads * dk // dk
    in_specs = [
        pl.BlockSpec((1, t, dk), lambda bi, h: (bi, 0, h)),
        pl.BlockSpec((1, t, dk), lambda bi, h: (bi, 0, heads + h)),
        pl.BlockSpec((1, t, dv), lambda bi, h: (bi, 0, (2 * heads * dk) // dv + h)),
        pl.BlockSpec((1, t, dv), lambda bi, h: (bi, 0, (2 * heads * dk) // dv + heads + h)),
        pl.BlockSpec((1, t, 2 * LANES), lambda bi, h: (bi, 0, col_l1 // (2 * LANES))),
        pl.BlockSpec((2, LANES, dk), lambda bi, h: (0, 0, h)),
        pl.BlockSpec((2, 1, dk), lambda bi, h: (0, 0, h)),
        pl.BlockSpec((1, dv), lambda bi, h: (0, 0)),
    ]
    args = [proj, proj, proj, proj, proj, a2p, ab, gnorm]
    if has_s0:
        in_specs.append(pl.BlockSpec((1, 2, 1, dk, dv), lambda bi, h: (bi, 0, h, 0, 0)))
        args.append(s0)
    out_specs = [pl.BlockSpec((1, t, dv), lambda bi, h: (bi, 0, h))]
    out_shape = [jax.ShapeDtypeStruct((b, t, heads * dv), BF16)]
    if want_sf:
        out_specs.append(pl.BlockSpec((1, 2, 1, dk, dv), lambda bi, h: (bi, 0, h, 0, 0)))
        out_shape.append(jax.ShapeDtypeStruct((b, 2, heads, dk, dv), F32))
    del kq
    res = pl.pallas_call(
        functools.partial(_gla_kernel, seq=t, chunk=GLA_CHUNK, dk=dk, has_s0=has_s0, want_sf=want_sf),
        grid=(b, heads),
        in_specs=in_specs,
        out_specs=out_specs,
        out_shape=out_shape,
        scratch_shapes=[pltpu.VMEM((t, dv), F32)],
        compiler_params=_cparams(2),
        name="gla_chunked",
    )(*args)
    return res if want_sf else (res[0], None)


def _rwkv_kernel(*refs, seq, chunk, want_sf):
    (r_ref, k_ref, v_ref, l1_ref, l1g_ref, w2_ref, a2_ref, g2_ref, w0_ref, a0_ref, mu_ref,
     kkw_ref, ka_ref, rk_ref, lng_ref, lnb_ref, s0_ref) = refs[:17]
    pos = 17
    o_ref = refs[pos]
    pos += 1
    sf_ref = None
    if want_sf:
        sf_ref = refs[pos]
        pos += 1
    rkv_s, rt_s, ys_s, g_s, ds_s, st_s = refs[pos:pos + 6]

    c = chunk
    c2 = 2 * chunk
    n_chunks = seq // c
    hn = RWKV_N

    lane = lax.broadcasted_iota(jnp.int32, (1, LANES), 1)
    head0 = lane < hn
    rowc = lax.broadcasted_iota(jnp.int32, (c, 1), 0)
    srow = lax.broadcasted_iota(jnp.int32, (c2, 1), 0)
    stack_mask = _div_pow2(srow, c) == _div_pow2(lane, hn)
    r2 = lax.broadcasted_iota(jnp.int32, (c2, c2), 0)
    q2 = lax.broadcasted_iota(jnp.int32, (c2, c2), 1)
    same_head = _div_pow2(r2, c) == _div_pow2(q2, c)
    rr = jnp.where(r2 >= c, r2 - c, r2)
    qq = jnp.where(q2 >= c, q2 - c, q2)
    eye2 = jnp.where(r2 == q2, 1.0, 0.0).astype(F32)
    lr = lax.broadcasted_iota(jnp.int32, (LANES, LANES), 0)
    lc = lax.broadcasted_iota(jnp.int32, (LANES, LANES), 1)
    diag_l = lr == lc
    block_l = _div_pow2(lr, hn) == _div_pow2(lc, hn)
    crow = lax.broadcasted_iota(jnp.int32, (c, c), 0)
    ccol = lax.broadcasted_iota(jnp.int32, (c, c), 1)

    def head_sum(x):
        s0 = jnp.sum(jnp.where(head0, x, 0.0), axis=-1, keepdims=True)
        s1 = jnp.sum(jnp.where(head0, 0.0, x), axis=-1, keepdims=True)
        return jnp.where(head0, s0, s1)

    def stack(x):
        return jnp.where(stack_mask, jnp.concatenate([x, x], axis=0), 0.0)

    def unstack(xs):
        return xs[:c] + xs[c:]

    mu = mu_ref[...]
    kkw = kkw_ref[...]
    ka = ka_ref[...]

    def shifted(ref, j, start, is_first, is_last):
        x = ref[0, pl.ds(start, c), :]
        pstart = pl.multiple_of(jnp.maximum(start - 8, 0), 8)
        nstart = pl.multiple_of(jnp.minimum(start + c, seq - 8), 8)
        prow = jnp.where(is_first, 0.0, ref[0, pl.ds(pstart, 8), :][7:8])
        nrow = jnp.where(is_last, 0.0, ref[0, pl.ds(nstart, 8), :][0:1])
        xp = jnp.where(rowc == 0, prow, pltpu.roll(x, 1, 0))
        xn = jnp.where(rowc == c - 1, nrow, pltpu.roll(x, c - 1, 0))
        return x + mu[0, j] * (xp - x) + mu[1, j] * (xn - x)

    tri_bf = [jnp.where(ccol <= crow, 1.0, 0.0).astype(BF16), jnp.where(ccol >= crow, 1.0, 0.0).astype(BF16)]
    n_levels = c.bit_length() - 1

    def level_mask(lvl, d):
        s = 1 << lvl
        siblings = jnp.logical_and(_div_pow2(rr, 2 * s) == _div_pow2(qq, 2 * s), same_head)
        r_late = (_div_pow2(rr, s) & 1) == 1
        q_late = (_div_pow2(qq, s) & 1) == 1
        if d == 0:
            return jnp.logical_and(siblings, jnp.logical_and(r_late, jnp.logical_not(q_late)))
        return jnp.logical_and(siblings, jnp.logical_and(jnp.logical_not(r_late), q_late))

    level_m = [[level_mask(lvl, d) for lvl in range(n_levels)] for d in range(2)]
    strict_m = [jnp.logical_and(qq < rr, same_head), jnp.logical_and(qq > rr, same_head)]
    incl_m = [jnp.logical_and(qq <= rr, same_head), jnp.logical_and(qq >= rr, same_head)]

    def phase_a(i, carry):
        chains = []
        for u in range(unroll):
            n = i * unroll + u
            start = pl.multiple_of(n * c, c)
            rows = pl.ds(start, c)
            is_first = n == 0
            is_last = n == n_chunks - 1
            r = shifted(r_ref, 0, start, is_first, is_last)
            k = shifted(k_ref, 1, start, is_first, is_last)
            v = shifted(v_ref, 2, start, is_first, is_last)
            rkv_s[0, rows, :] = r
            rkv_s[1, rows, :] = k
            rkv_s[2, rows, :] = v
            kk = k * kkw
            kk = kk * lax.rsqrt(head_sum(kk * kk) + 1e-12)
            vs_h = stack(v).astype(BF16)
            for d in range(2):
                chains.append(dict(n=n, rows=rows, d=d, r=r, k=k, kk=kk, vs_h=vs_h))

        for ch in chains:
            d, rows = ch["d"], ch["rows"]
            l1w = l1_ref[0, rows, d * LANES:(d + 1) * LANES]
            l1a = l1_ref[0, rows, (2 + d) * LANES:(3 + d) * LANES]
            ch["dlog"] = w0_ref[d] + _dot(jnp.tanh(l1w).astype(BF16), w2_ref[d])
            ch["alog"] = a0_ref[d] + _dot(l1a.astype(BF16), a2_ref[d])
        for ch in chains:
            dlog = ch["dlog"]
            sp = jnp.maximum(-dlog, 0.0) + jnp.log(1.0 + jnp.exp(-jnp.abs(dlog)))
            ch["lw"] = -jnp.exp(-sp - 0.5)
            a = _sigmoid(ch["alog"])
            ch["kd"] = ch["k"] * (1.0 + (a - 1.0) * ka)
            ch["bb"] = a * ch["kk"]
        for ch in chains:
            ch["lcum"] = _dot_exact_lhs(tri_bf[ch["d"]], ch["lw"])
        for ch in chains:
            d, lcum = ch["d"], ch["lcum"]
            ltot = lcum[c - 1:c] if d == 0 else lcum[0:1]
            e_nl = jnp.exp(-lcum)
            e_rem = jnp.exp(ltot - lcum)
            ch["e_tot"] = jnp.exp(ltot)
            ch["rs"] = stack(ch["r"] * jnp.exp(lcum))
            ch["ks_b"] = stack(ch["kk"] * jnp.exp(lcum - ch["lw"])).astype(BF16)
            kbs_b = stack(ch["kd"] * e_nl).astype(BF16)
            bbs_b = stack(ch["bb"] * e_nl).astype(BF16)
            ch["kts_b"] = stack(ch["kd"] * e_rem).astype(BF16)
            ch["bts_b"] = stack(ch["bb"] * e_rem).astype(BF16)
            ch["lhs"] = jnp.concatenate([ch["ks_b"], ch["rs"].astype(BF16)], axis=0)
            ch["rhs"] = jnp.concatenate([kbs_b, bbs_b], axis=0)
        for ch in chains:
            ch["aa"] = _dot(ch["lhs"], ch["rhs"], _NT)
        for ch in chains:
            d, aa = ch["d"], ch["aa"]
            ch["a_kk"] = jnp.where(strict_m[d], aa[:c2, :c2], 0.0).astype(BF16)
            ch["a_rk"] = jnp.where(incl_m[d], aa[c2:, :c2], 0.0).astype(BF16)
            ch["a_rb"] = jnp.where(incl_m[d], aa[c2:, c2:], 0.0).astype(BF16)
            ch["a_kb"] = aa[:c2, c2:]
            ch["tinv"] = eye2 - jnp.where(level_m[d][0], ch["a_kb"], 0.0)
        for ch in chains:
            ch["av"] = _dot(ch["a_kk"], ch["vs_h"]).astype(BF16)
        for ch in chains:
            ch["vk"] = _dot(ch["vs_h"], ch["kts_b"], _TN)
        for lvl in range(1, n_levels):
            for ch in chains:
                ch["tb"] = ch["tinv"].astype(BF16)
                coupling = jnp.where(level_m[ch["d"]][lvl], ch["a_kb"], 0.0).astype(BF16)
                ch["e"] = _dot(coupling, ch["tb"]).astype(BF16)
            for ch in chains:
                ch["tinv"] = ch["tinv"] - _dot(ch["tb"], ch["e"])
        for ch in chains:
            wu = _dot(ch["tinv"].astype(BF16), jnp.concatenate([ch["ks_b"], ch["av"]], axis=1))
            ch["wu_b"] = wu.astype(BF16)
        for ch in chains:
            ch["ru"] = _dot(ch["a_rb"], ch["wu_b"])
            ch["y0"] = _dot(ch["a_rk"], ch["vs_h"])
        for ch in chains:
            ch["wb"] = _dot(ch["wu_b"], ch["bts_b"], _TN)
        y0_sum = {}
        for ch in chains:
            d, n, rows = ch["d"], ch["n"], ch["rows"]
            rt = ch["rs"] - ch["ru"][:, :LANES]
            y0 = ch["y0"] - ch["ru"][:, LANES:]
            gmat = jnp.where(diag_l, ch["e_tot"], 0.0) - jnp.where(block_l, ch["wb"][:LANES], 0.0)
            dmat = jnp.where(block_l, ch["vk"] - ch["wb"][LANES:], 0.0)
            rt_s[d, rows, :] = unstack(rt).astype(BF16)
            g_s[d, n] = gmat.astype(BF16)
            ds_s[d, n] = dmat
            if d == 0:
                y0_sum = unstack(y0)
            else:
                ys_s[rows, :] = y0_sum + unstack(y0)
        return carry

    unroll = 8 if n_chunks % 16 == 0 else (4 if n_chunks % 4 == 0 else (2 if n_chunks % 2 == 0 else 1))
    lax.fori_loop(0, n_chunks // unroll, phase_a, 0)

    st_s[0] = s0_ref[0, 0, 0]
    st_s[1] = s0_ref[0, 1, 0]

    def phase_b(i, carry):
        for d in range(2):
            idx = i if d == 0 else n_chunks - 1 - i
            rows = pl.ds(pl.multiple_of(idx * c, c), c)
            st_b = st_s[d].astype(BF16)
            ys_s[rows, :] = ys_s[rows, :] + _dot(rt_s[d, rows, :], st_b, _NT)
            st_s[d] = _dot(st_b, g_s[d, idx]) + ds_s[d, idx]
        return carry

    lax.fori_loop(0, n_chunks, phase_b, 0)
    if want_sf:
        sf_ref[0, 0, 0] = st_s[0]
        sf_ref[0, 1, 0] = st_s[1]

    rb = min(seq, 256)
    lng = lng_ref[...]
    lnb = lnb_ref[...]
    rkw = rk_ref[...]
    g2 = g2_ref[...]

    def epilogue(i, carry):
        rows = pl.ds(pl.multiple_of(i * rb, rb), rb)
        y = ys_s[rows, :]
        mean = head_sum(y) * (1.0 / hn)
        yc = y - mean
        var = head_sum(yc * yc) * (1.0 / hn)
        yn = yc * lax.rsqrt(var + RWKV_LN_EPS) * lng + lnb
        r = rkv_s[0, rows, :]
        k = rkv_s[1, rows, :]
        v = rkv_s[2, rows, :]
        bonus = head_sum(r * k * rkw) * v
        gate = _dot(_sigmoid(l1g_ref[0, rows, :]).astype(BF16), g2)
        o_ref[0, rows, :] = ((yn + bonus) * gate).astype(o_ref.dtype)
        return carry

    lax.fori_loop(0, seq // rb, epilogue, 0)


def _rwkv(proj, w2p, a2p, g2, w0, a0, mu, kkw, ka, rkw, lng, lnb, s0_bd, want_sf, col_rkv, col_l1, col_g):
    b, t, _ = proj.shape
    width = kkw.shape[-1]
    pairs = width // LANES
    c = RWKV_CHUNK
    n_chunks = t // c
    cb = col_rkv // LANES
    vec = lambda: pl.BlockSpec((1, LANES), lambda bi, h: (0, h))
    in_specs = [
        pl.BlockSpec((1, t, LANES), lambda bi, h: (bi, 0, cb + h)),
        pl.BlockSpec((1, t, LANES), lambda bi, h: (bi, 0, cb + pairs + h)),
        pl.BlockSpec((1, t, LANES), lambda bi, h: (bi, 0, cb + 2 * pairs + h)),
        pl.BlockSpec((1, t, 4 * LANES), lambda bi, h: (bi, 0, col_l1 // (4 * LANES))),
        pl.BlockSpec((1, t, 2 * LANES), lambda bi, h: (bi, 0, col_g // (2 * LANES))),
        pl.BlockSpec((2, LANES, LANES), lambda bi, h: (0, 0, h)),
        pl.BlockSpec((2, LANES, LANES), lambda bi, h: (0, 0, h)),
        pl.BlockSpec((2 * LANES, LANES), lambda bi, h: (0, h)),
        pl.BlockSpec((2, 1, LANES), lambda bi, h: (0, 0, h)),
        pl.BlockSpec((2, 1, LANES), lambda bi, h: (0, 0, h)),
        pl.BlockSpec((2, 3, 1, LANES), lambda bi, h: (0, 0, 0, h)),
        vec(), vec(), vec(), vec(), vec(),
        pl.BlockSpec((1, 2, 1, LANES, LANES), lambda bi, h: (bi, 0, h, 0, 0)),
    ]
    out_specs = [pl.BlockSpec((1, t, LANES), lambda bi, h: (bi, 0, h))]
    out_shape = [jax.ShapeDtypeStruct((b, t, width), BF16)]
    if want_sf:
        out_specs.append(pl.BlockSpec((1, 2, 1, LANES, LANES), lambda bi, h: (bi, 0, h, 0, 0)))
        out_shape.append(jax.ShapeDtypeStruct((b, 2, pairs, LANES, LANES), F32))
    res = pl.pallas_call(
        functools.partial(_rwkv_kernel, seq=t, chunk=c, want_sf=want_sf),
        grid=(b, pairs),
        in_specs=in_specs,
        out_specs=out_specs,
        out_shape=out_shape,
        scratch_shapes=[pltpu.VMEM((3, t, LANES), F32),
                        pltpu.VMEM((2, t, LANES), BF16),
                        pltpu.VMEM((t, LANES), F32),
                        pltpu.VMEM((2, n_chunks, LANES, LANES), BF16),
                        pltpu.VMEM((2, n_chunks, LANES, LANES), F32),
                        pltpu.VMEM((2, LANES, LANES), F32)],
        compiler_params=_cparams(2),
        name="rwkv7_chunked",
    )(proj, proj, proj, proj, proj, w2p, a2p, g2, w0, a0, mu, kkw, ka, rkw, lng, lnb, s0_bd)
    return res if want_sf else (res[0], None)


def _rope(x, cos, sin, first_half):
    part = jnp.where(first_half, pltpu.roll(x, x.shape[-1] - 32, 1), pltpu.roll(x, 32, 1))
    return x * cos + part * sin


def _diff_attn_kernel(*refs, rope, has_cache, lam_init, tq):
    q_ref, k_ref, v_ref, lq_ref, lk_ref, sub_ref = refs[:6]
    pos = 6
    if rope:
        cos_ref, sin_ref = refs[pos:pos + 2]
        pos += 2
    if has_cache:
        ck_ref, cv_ref = refs[pos:pos + 2]
        pos += 2
    o_ref = refs[pos]
    ks_ref, vs_ref = refs[pos + 1:pos + 3]

    t = k_ref.shape[2]
    w = k_ref.shape[3]
    dh = DIFF_DH
    i = pl.program_id(2)
    lane = lax.broadcasted_iota(jnp.int32, (1, w), 1)
    first_half = (lane & 63) < 32

    @pl.when(i == 0)
    def _():
        kb = min(t, 512)

        def fill(j, carry):
            rows = pl.ds(pl.multiple_of(j * kb, kb), kb)
            kx = k_ref[0, 0, rows, :]
            if rope:
                kx = _rope(kx, cos_ref[rows, :], sin_ref[rows, :], first_half)
            ks_ref[rows, :] = kx.astype(BF16)
            vs_ref[rows, :] = v_ref[0, 0, rows, :].astype(BF16)
            return carry

        lax.fori_loop(0, t // kb, fill, 0)
        if has_cache:
            ks_ref[t:, :] = ck_ref[0, 0, 0].astype(BF16)
            vs_ref[t:, :] = cv_ref[0, 0, 0].astype(BF16)

    q = q_ref[0, 0]
    if rope:
        qrows = pl.ds(pl.multiple_of(i * tq, tq), tq)
        q = _rope(q, cos_ref[qrows, :], sin_ref[qrows, :], first_half)
    qb = q.astype(BF16)
    lq = lq_ref[...]
    lk = lk_ref[...]
    lam = (jnp.exp(jnp.sum(lq[0:1] * lk[0:1], axis=-1, keepdims=True))
           - jnp.exp(jnp.sum(lq[1:2] * lk[1:2], axis=-1, keepdims=True)) + lam_init)
    kall = ks_ref[...]
    vall = vs_ref[...]
    exp2_scale = dh ** -0.5 * math.log2(math.e)
    sub = min(tq, 128)
    scores = [[_dot(qb[r:r + sub, m * dh:(m + 1) * dh], kall[:, m * dh:(m + 1) * dh], _NT) for m in range(2)]
              for r in range(0, tq, sub)]
    rows_out = []
    for pair in scores:
        outs = []
        for s in pair:
            e = jnp.exp2((s - jnp.max(s, axis=-1, keepdims=True)) * exp2_scale)
            inv = 1.0 / jnp.sum(e, axis=-1, keepdims=True)
            outs.append(_dot(e.astype(BF16), vall) * inv)
        rows_out.append(outs[0] - lam * outs[1])
    o = jnp.concatenate(rows_out, axis=0) if len(rows_out) > 1 else rows_out[0]
    on = o * lax.rsqrt(jnp.mean(o * o, axis=-1, keepdims=True) + SUBLN_EPS) * sub_ref[...]
    o_ref[0] = (on * (1.0 - lam_init)).astype(o_ref.dtype)


def _diff_attention(qkv, heads, lq, lk, subln, lam_init, rope_tabs, cache_k, cache_v, tq=512):
    b, _, t, w = qkv.shape
    tq = _pick(t, tq)
    rope = rope_tabs is not None
    has_cache = cache_k is not None
    tk = t + (cache_k.shape[3] if has_cache else 0)
    in_specs = [
        pl.BlockSpec((1, 1, tq, w), lambda bi, h, i: (bi, h, i, 0)),
        pl.BlockSpec((1, 1, t, w), lambda bi, h, i: (bi, heads + h, 0, 0)),
        pl.BlockSpec((1, 1, t, w), lambda bi, h, i: (bi, 2 * heads + h, 0, 0)),
        pl.BlockSpec((2, DIFF_DH), lambda bi, h, i: (0, 0)),
        pl.BlockSpec((2, DIFF_DH), lambda bi, h, i: (0, 0)),
        pl.BlockSpec((1, w), lambda bi, h, i: (0, 0)),
    ]
    args = [qkv, qkv, qkv, lq, lk, subln.reshape(1, w)]
    if rope:
        in_specs += [pl.BlockSpec((t, w), lambda bi, h, i: (0, 0))] * 2
        args += list(rope_tabs)
    if has_cache:
        pc = cache_k.shape[3]
        in_specs += [pl.BlockSpec((1, 1, 1, pc, w), lambda bi, h, i: (bi, 0, h, 0, 0))] * 2
        args += [cache_k, cache_v]
    return pl.pallas_call(
        functools.partial(_diff_attn_kernel, rope=rope, has_cache=has_cache, lam_init=lam_init, tq=tq),
        grid=(b, heads, t // tq),
        in_specs=in_specs,
        out_specs=pl.BlockSpec((1, tq, w), lambda bi, h, i: (bi, i, h)),
        out_shape=jax.ShapeDtypeStruct((b, t, heads * w), BF16),
        scratch_shapes=[pltpu.VMEM((tk, w), BF16), pltpu.VMEM((tk, w), BF16)],
        compiler_params=_cparams(3),
        name="diff_attention",
    )(*args)


def _rope_tables(t, w):
    nf = DIFF_DH // 4
    rows = t // GRID_W
    row = jnp.repeat(jnp.arange(rows, dtype=F32), GRID_W)
    col = jnp.tile(jnp.arange(GRID_W, dtype=F32), rows)
    inv = jnp.power(ROPE_BASE, -jnp.arange(nf, dtype=F32) / nf)
    ang_r = row[:, None] * inv
    ang_c = col[:, None] * inv
    cos = jnp.concatenate([jnp.cos(ang_r), jnp.cos(ang_r), jnp.cos(ang_c), jnp.cos(ang_c)], axis=-1)
    sin = jnp.concatenate([-jnp.sin(ang_r), jnp.sin(ang_r), -jnp.sin(ang_c), jnp.sin(ang_c)], axis=-1)
    reps = w // DIFF_DH
    return jnp.tile(cos, (1, reps)), jnp.tile(sin, (1, reps))


def _pad_rows(w, rows):
    return jnp.pad(w, ((0, 0),) * (w.ndim - 2) + ((0, rows - w.shape[-2]), (0, 0)))


def _pad_cols(w, cols):
    return jnp.pad(w, ((0, 0),) * (w.ndim - 1) + ((0, cols - w.shape[-1]),))


def kernel(x_prompt, x_sample, state_gla, state_rwkv, cache_k, cache_v, c, c_ctx, ada_w, ada_b, norm_mix, norm_ffn, norm_final, ffn_w1, ffn_w3, ffn_w2, even_w_in, even_w_out, gla_a1, gla_a2, gla_ab, gla_norm, rw_mu, rw_w0, rw_w1, rw_w2, rw_a0, rw_a1, rw_a2, rw_g1, rw_g2, rw_kk, rw_ka, rw_rk, rw_ln_g, rw_ln_b, odd_w_in, odd_w_out, diff_lq, diff_lk, diff_subln):
    depth, d, _ = ada_w.shape
    dec_b = x_sample.shape[0]
    gla_heads, gla_dk, gla_dv = state_gla.shape[3:]
    rw_heads = state_rwkv.shape[3]
    rw_w = rw_heads * RWKV_N
    pairs = rw_w // LANES
    diff_heads, _, diff_w = cache_k.shape[2:]
    gla_qk = gla_heads * gla_dk
    gla_vw = gla_heads * gla_dv
    even_in = even_w_in.shape[-1]

    nb = -(-(1 + dec_b) // 8) * 8
    cond = jnp.zeros((nb, d), F32).at[0].set(c_ctx).at[1:1 + dec_b].set(c)
    mod = _adaln(cond, ada_w, ada_b)

    def mods(layer, prompt):
        m = mod[layer, 0:1] if prompt else mod[layer, 1:1 + dec_b]
        return [m[:, None, k * d:(k + 1) * d] for k in range(6)]

    def even_weights(e):
        lora1 = jnp.concatenate(
            [_pad_cols(rw_w1[e, 0], LANES), _pad_cols(rw_w1[e, 1], LANES),
             _pad_cols(rw_a1[e, 0], LANES), _pad_cols(rw_a1[e, 1], LANES),
             rw_g1[e],
             _pad_cols(gla_a1[e, 0], LANES), _pad_cols(gla_a1[e, 1], LANES)], axis=-1)
        w_cat = jnp.concatenate([even_w_in[e], lora1], axis=-1).astype(BF16)
        return dict(
            w_cat=w_cat,
            w_out=even_w_out[e],
            gla_a2=_pad_rows(gla_a2[e], LANES),
            gla_ab=gla_ab[e].reshape(2, 1, gla_qk),
            gla_norm=gla_norm[e].reshape(1, gla_dv),
            w2=_pad_rows(rw_w2[e], LANES).astype(BF16),
            a2=_pad_rows(rw_a2[e], LANES).astype(BF16),
            g2=rw_g2[e].astype(BF16),
            w0=rw_w0[e].reshape(2, 1, rw_w),
            a0=rw_a0[e].reshape(2, 1, rw_w),
            mu=rw_mu[e].reshape(2, 3, 1, rw_w),
            kk=rw_kk[e].reshape(1, rw_w), ka=rw_ka[e].reshape(1, rw_w), rk=rw_rk[e].reshape(1, rw_w),
            lng=rw_ln_g[e].reshape(1, rw_w), lnb=rw_ln_b[e].reshape(1, rw_w),
        )

    col_rkv = 2 * gla_qk + 2 * gla_vw
    col_l1 = even_in
    col_g = even_in + 4 * LANES
    col_gl = even_in + 6 * LANES

    def block_diag_state(s):
        bsz = s.shape[0]
        s = s.reshape(bsz, 2, pairs, 2, RWKV_N, RWKV_N)
        z = jnp.zeros_like(s[:, :, :, 0])
        top = jnp.concatenate([s[:, :, :, 0], z], axis=-1)
        bot = jnp.concatenate([z, s[:, :, :, 1]], axis=-1)
        return jnp.concatenate([top, bot], axis=-2)

    def diag_blocks(sbd):
        bsz = sbd.shape[0]
        h0 = sbd[:, :, :, :RWKV_N, :RWKV_N]
        h1 = sbd[:, :, :, RWKV_N:, RWKV_N:]
        return jnp.stack([h0, h1], axis=3).reshape(bsz, 2, rw_heads, RWKV_N, RWKV_N)

    def even_mixer(h, ew, s_gla, s_rwkv, want_states):
        proj = _matmul(h, ew["w_cat"], name="even_in_proj")
        o_gla, sg = _gla(proj, ew["gla_a2"], ew["gla_ab"], ew["gla_norm"], s_gla, want_states,
                         gla_heads, gla_dk, gla_dv, col_gl)
        if s_rwkv is None:
            s_bd = jnp.zeros((h.shape[0], 2, pairs, LANES, LANES), F32)
        else:
            s_bd = block_diag_state(s_rwkv)
        y_rw, sr = _rwkv(proj, ew["w2"], ew["a2"], ew["g2"], ew["w0"], ew["a0"], ew["mu"], ew["kk"], ew["ka"],
                         ew["rk"], ew["lng"], ew["lnb"], s_bd, want_states, col_rkv, col_l1, col_g)
        mix = jnp.concatenate([o_gla, y_rw], axis=-1)
        return mix, ew["w_out"], sg, (diag_blocks(sr) if want_states else None)

    def run_group(x, prompt):
        outs = {}
        for layer in range(depth):
            sh1, sc1, g1, sh2, sc2, g2 = mods(layer, prompt)
            h = _norm_mod(x, norm_mix[layer], sc1, sh1)
            if layer % 2 == 0:
                e = layer // 2
                ew = even_weights(e)
                if prompt:
                    mix, w_out, sg, sr = even_mixer(h, ew, None, None, True)
                    outs.setdefault("gla", []).append(sg)
                    outs.setdefault("rwkv", []).append(sr)
                else:
                    mix, w_out, _, _ = even_mixer(h, ew, state_gla[:, e], state_rwkv[:, e], False)
            else:
                o = layer // 2
                lam_init = 0.8 - 0.6 * math.exp(-0.3 * layer)
                qkv = _matmul_heads(h, odd_w_in[o], diff_w)
                if prompt:
                    outs.setdefault("k", []).append(qkv[:, diff_heads:2 * diff_heads])
                    outs.setdefault("v", []).append(qkv[:, 2 * diff_heads:])
                    mix = _diff_attention(qkv, diff_heads, diff_lq[o], diff_lk[o], diff_subln[o], lam_init,
                                          None, None, None)
                else:
                    mix = _diff_attention(qkv, diff_heads, diff_lq[o], diff_lk[o], diff_subln[o], lam_init,
                                          _rope_tables(x.shape[1], diff_w), cache_k[:, o:o + 1], cache_v[:, o:o + 1])
                w_out = odd_w_out[o]
            x = _matmul_residual(mix, w_out, x, g1, name="mix_out_proj")
            h = _norm_mod(x, norm_ffn[layer], sc2, sh2)
            u = _matmul_swiglu(h, ffn_w1[layer], ffn_w3[layer])
            x = _matmul_residual(u, ffn_w2[layer], x, g2, tn=256, name="ffn_down_proj")
        return _final_norm(x, norm_final), outs

    y_prompt, po = run_group(x_prompt, True)
    y_sample, _ = run_group(x_sample, False)
    return (y_prompt, y_sample,
            jnp.stack(po["gla"], axis=1), jnp.stack(po["rwkv"], axis=1),
            jnp.stack(po["k"], axis=1), jnp.stack(po["v"], axis=1))
```

```python
import functools
import math

import jax
import jax.numpy as jnp
from jax import lax
from jax.experimental import pallas as pl
from jax.experimental.pallas import tpu as pltpu

F32 = jnp.float32
BF16 = jnp.bfloat16

V7X_VMEM_LIMIT_BYTES = 56 * 1024 * 1024
LANES = 128

NORM_EPS = 1e-6
GLA_GATE_NORM = 16.0
GLA_CHUNK = 64
RWKV_CHUNK = 64
RWKV_N = 64
RWKV_LN_EPS = 64e-5
DIFF_DH = 128
GRID_W = 64
ROPE_BASE = 10000.0
SUBLN_EPS = 1e-5


def _cparams(n_axes):
    return pltpu.CompilerParams(dimension_semantics=("arbitrary",) * n_axes,
                                vmem_limit_bytes=V7X_VMEM_LIMIT_BYTES)


def _dot(a, b, dims=(((1,), (0,)), ((), ()))):
    return lax.dot_general(a, b, dims, preferred_element_type=F32)


_NN = (((1,), (0,)), ((), ()))
_NT = (((1,), (1,)), ((), ()))
_TN = (((0,), (0,)), ((), ()))


def _split3(x):
    h = x.astype(BF16)
    r = x - h.astype(F32)
    m = r.astype(BF16)
    l = (r - m.astype(F32)).astype(BF16)
    return h, m, l


def _split2(x):
    h = x.astype(BF16)
    m = (x - h.astype(F32)).astype(BF16)
    return h, m


def _dot_hp(a, b, dims=_NN):
    ah, al = _split2(a)
    bh, bl = _split2(b)
    return _dot(ah, bh, dims) + (_dot(ah, bl, dims) + _dot(al, bh, dims))


def _dot_exact_lhs(a_bf, b, dims=_NN):
    h, m, l = _split3(b)
    return _dot(a_bf, h, dims) + (_dot(a_bf, m, dims) + _dot(a_bf, l, dims))


def _div_pow2(x, n):
    assert n & (n - 1) == 0
    return lax.shift_right_logical(x, n.bit_length() - 1)


def _sigmoid(x):
    return 1.0 / (1.0 + jnp.exp(-x))


def _log_sigmoid(z):
    return jnp.minimum(z, 0.0) - jnp.log(1.0 + jnp.exp(-jnp.abs(z)))


def _adaln_kernel(c_ref, w_ref, b_ref, o_ref):
    cs = c_ref[...]
    s = cs * _sigmoid(cs)
    o_ref[0] = _dot(s.astype(BF16), w_ref[0].astype(BF16)) + b_ref[0]


def _adaln(cond, ada_w, ada_b, tn=1024):
    depth, d, n = ada_w.shape
    nb = cond.shape[0]
    return pl.pallas_call(
        _adaln_kernel,
        grid=(depth, n // tn),
        in_specs=[pl.BlockSpec((nb, d), lambda l, j: (0, 0)),
                  pl.BlockSpec((1, d, tn), lambda l, j: (l, 0, j)),
                  pl.BlockSpec((1, 1, tn), lambda l, j: (l, 0, j))],
        out_specs=pl.BlockSpec((1, nb, tn), lambda l, j: (l, 0, j)),
        out_shape=jax.ShapeDtypeStruct((depth, nb, n), F32),
        compiler_params=_cparams(2),
        name="adaln_mod",
    )(cond, ada_w, ada_b.reshape(depth, 1, n))


def _norm_mod_kernel(x_ref, g_ref, sc_ref, sh_ref, o_ref):
    x = x_ref[0]
    xn = x * lax.rsqrt(jnp.mean(x * x, axis=-1, keepdims=True) + NORM_EPS)
    o_ref[0] = ((xn * g_ref[...]) * (1.0 + sc_ref[0]) + sh_ref[0]).astype(o_ref.dtype)


def _norm_mod(x, g, sc, sh, tt=1024):
    b, t, d = x.shape
    tt = _pick(t, tt)
    per_batch = sc.shape[0] > 1
    mod_map = (lambda i, j: (i, 0, 0)) if per_batch else (lambda i, j: (0, 0, 0))
    return pl.pallas_call(
        _norm_mod_kernel,
        grid=(b, t // tt),
        in_specs=[pl.BlockSpec((1, tt, d), lambda i, j: (i, j, 0)),
                  pl.BlockSpec((1, d), lambda i, j: (0, 0)),
                  pl.BlockSpec((1, 1, d), mod_map),
                  pl.BlockSpec((1, 1, d), mod_map)],
        out_specs=pl.BlockSpec((1, tt, d), lambda i, j: (i, j, 0)),
        out_shape=jax.ShapeDtypeStruct((b, t, d), BF16),
        compiler_params=_cparams(2),
        name="norm_mod",
    )(x, g.reshape(1, d), sc, sh)


def _final_norm_kernel(x_ref, g_ref, o_ref):
    x = x_ref[0]
    xn = x * lax.rsqrt(jnp.mean(x * x, axis=-1, keepdims=True) + NORM_EPS)
    o_ref[0] = xn * g_ref[...]


def _final_norm(x, g, tt=1024):
    b, t, d = x.shape
    tt = _pick(t, tt)
    return pl.pallas_call(
        _final_norm_kernel,
        grid=(b, t // tt),
        in_specs=[pl.BlockSpec((1, tt, d), lambda i, j: (i, j, 0)),
                  pl.BlockSpec((1, d), lambda i, j: (0, 0))],
        out_specs=pl.BlockSpec((1, tt, d), lambda i, j: (i, j, 0)),
        out_shape=jax.ShapeDtypeStruct((b, t, d), F32),
        compiler_params=_cparams(2),
        name="final_norm",
    )(x, g.reshape(1, d))


def _mm_kernel(x_ref, w_ref, o_ref):
    o_ref[...] = _dot(x_ref[...], w_ref[...]).astype(o_ref.dtype)


def _pick(n, pref):
    t = min(n, pref)
    while n % t:
        t //= 2
    return t


def _matmul(x, w, out_dtype=F32, tm=1024, tn=512, name="matmul"):
    b, t, k = x.shape
    n = w.shape[1]
    m = b * t
    tm, tn = _pick(m, tm), _pick(n, tn)
    out = pl.pallas_call(
        _mm_kernel,
        grid=(m // tm, n // tn),
        in_specs=[pl.BlockSpec((tm, k), lambda i, j: (i, 0)),
                  pl.BlockSpec((k, tn), lambda i, j: (0, j))],
        out_specs=pl.BlockSpec((tm, tn), lambda i, j: (i, j)),
        out_shape=jax.ShapeDtypeStruct((m, n), out_dtype),
        compiler_params=_cparams(2),
        name=name,
    )(x.reshape(m, k), w)
    return out.reshape(b, t, n)


def _mm_res_kernel(*refs, n_in):
    r_ref, g_ref, o_ref = refs[2 * n_in:]
    acc = _dot(refs[0][...], refs[n_in][...])
    for i in range(1, n_in):
        acc = acc + _dot(refs[i][...], refs[n_in + i][...])
    o_ref[...] = r_ref[...] + g_ref[0] * acc


def _matmul_residual(xs, w, res, gate, tm=1024, tn=512, name="matmul_residual"):
    xs = xs if isinstance(xs, (list, tuple)) else [xs]
    b, t, k = xs[0].shape
    n_in = len(xs)
    n = w.shape[1]
    m = b * t
    per_batch = gate.shape[0] > 1
    tm = _pick(t if per_batch else m, tm)
    tn = _pick(n, tn)
    gate_map = (lambda i, j: ((i * tm) // t, 0, j)) if per_batch else (lambda i, j: (0, 0, j))
    out = pl.pallas_call(
        functools.partial(_mm_res_kernel, n_in=n_in),
        grid=(m // tm, n // tn),
        in_specs=([pl.BlockSpec((tm, k), lambda i, j: (i, 0))] * n_in
                  + [pl.BlockSpec((k, tn), functools.partial(lambda i, j, p: (p, j), p=p)) for p in range(n_in)]
                  + [pl.BlockSpec((tm, tn), lambda i, j: (i, j)),
                     pl.BlockSpec((1, 1, tn), gate_map)]),
        out_specs=pl.BlockSpec((tm, tn), lambda i, j: (i, j)),
        out_shape=jax.ShapeDtypeStruct((m, n), F32),
        compiler_params=_cparams(2),
        name=name,
    )(*[x.reshape(m, k) for x in xs], *([w] * n_in), res.reshape(m, n), gate)
    return out.reshape(b, t, n)


def _mm_swiglu_kernel(x_ref, w1_ref, w3_ref, o_ref):
    x = x_ref[...]
    a = _dot(x, w1_ref[...])
    o_ref[...] = ((a * _sigmoid(a)) * _dot(x, w3_ref[...])).astype(o_ref.dtype)


def _matmul_swiglu(x, w1, w3, tm=1024, tn=512):
    b, t, k = x.shape
    n = w1.shape[1]
    m = b * t
    tm, tn = _pick(m, tm), _pick(n, tn)
    out = pl.pallas_call(
        _mm_swiglu_kernel,
        grid=(m // tm, n // tn),
        in_specs=[pl.BlockSpec((tm, k), lambda i, j: (i, 0)),
                  pl.BlockSpec((k, tn), lambda i, j: (0, j)),
                  pl.BlockSpec((k, tn), lambda i, j: (0, j))],
        out_specs=pl.BlockSpec((tm, tn), lambda i, j: (i, j)),
        out_shape=jax.ShapeDtypeStruct((m, n), BF16),
        compiler_params=_cparams(2),
        name="swiglu_up",
    )(x.reshape(m, k), w1, w3)
    return out.reshape(b, t, n)


def _mm_heads_kernel(x_ref, wq_ref, wk_ref, wv_ref, q_ref, k_ref, v_ref, *, nbt, rows, hb, hw):
    x = x_ref[...]
    for w_ref, o_ref in ((wq_ref, q_ref), (wk_ref, k_ref), (wv_ref, v_ref)):
        r = _dot(x, w_ref[...])
        for bb in range(nbt):
            for i in range(hb):
                o_ref[bb, i] = r[bb * rows:(bb + 1) * rows, i * hw:(i + 1) * hw]


def _matmul_heads(x, w, heads, hw, hb=2, tm=1024):
    b, t, k = x.shape
    m = b * t
    tm = _pick(m, tm)
    if tm >= t:
        nbt, rows = tm // t, t
        out_map = lambda i, j: (i, j, 0, 0)
    else:
        nbt, rows = 1, tm
        per_seq = t // tm
        out_map = lambda i, j: (i // per_seq, j, i % per_seq, 0)
    hblocks = heads // hb
    out_spec = pl.BlockSpec((nbt, hb, rows, hw), out_map)
    return pl.pallas_call(
        functools.partial(_mm_heads_kernel, nbt=nbt, rows=rows, hb=hb, hw=hw),
        grid=(m // tm, hblocks),
        in_specs=[pl.BlockSpec((tm, k), lambda i, j: (i, 0)),
                  pl.BlockSpec((k, hb * hw), lambda i, j: (0, j)),
                  pl.BlockSpec((k, hb * hw), lambda i, j: (0, hblocks + j)),
                  pl.BlockSpec((k, hb * hw), lambda i, j: (0, 2 * hblocks + j))],
        out_specs=[out_spec] * 3,
        out_shape=[jax.ShapeDtypeStruct((b, heads, t, hw), F32)] * 3,
        compiler_params=_cparams(2),
        name="qkv_heads",
    )(x.reshape(m, k), w, w, w)


def _gla_kernel(*refs, seq, chunk, dk, has_s0, want_sf):
    q_ref, k_ref, v_ref, gg_ref, l1_ref, a2_ref, ab_ref, gn_ref = refs[:8]
    pos = 8
    s0_ref = None
    if has_s0:
        s0_ref = refs[pos]
        pos += 1
    o_ref = refs[pos]
    pos += 1
    sf_ref = None
    if want_sf:
        sf_ref = refs[pos]
        pos += 1
    acc_ref = refs[pos]

    n_chunks = seq // chunk
    dv = v_ref.shape[-1]
    scale = dk ** -0.5
    row = lax.broadcasted_iota(jnp.int32, (chunk, chunk), 0)
    col = lax.broadcasted_iota(jnp.int32, (chunk, chunk), 1)

    tri = [col <= row, col >= row]
    tri_bf = [jnp.where(t, 1.0, 0.0).astype(BF16) for t in tri]
    a2 = [a2_ref[d].astype(BF16) for d in range(2)]
    ab = [ab_ref[d] for d in range(2)]
    if has_s0:
        st0 = tuple(s0_ref[0, d, 0].T for d in range(2))
    else:
        st0 = tuple(jnp.zeros((dv, dk), F32) for d in range(2))
    unroll = 4 if n_chunks % 4 == 0 else (2 if n_chunks % 2 == 0 else 1)
    rb = min(seq, 256)

    def zero(i, carry):
        acc_ref[pl.ds(pl.multiple_of(i * rb, rb), rb), :] = jnp.zeros((rb, dv), F32)
        return carry

    lax.fori_loop(0, seq // rb, zero, 0)

    def body(i, carry):
        st = list(carry)
        chains = []
        for u in range(unroll):
            for d in range(2):
                n = i * unroll + u
                idx = n if d == 0 else n_chunks - 1 - n
                chains.append(dict(d=d, rows=pl.ds(pl.multiple_of(idx * chunk, chunk), chunk)))
        for ch in chains:
            d, rows = ch["d"], ch["rows"]
            l1 = l1_ref[0, rows, d * LANES:(d + 1) * LANES]
            ch["z"] = _dot(l1.astype(BF16), a2[d]) + ab[d]
        for ch in chains:
            ch["la"] = _log_sigmoid(ch["z"]) / GLA_GATE_NORM
        for ch in chains:
            ch["cum"] = _dot_exact_lhs(tri_bf[ch["d"]], ch["la"])
        for ch in chains:
            d, rows, cum = ch["d"], ch["rows"], ch["cum"]
            cl = cum[chunk - 1:chunk] if d == 0 else cum[0:1]
            k = k_ref[0, rows, :]
            ch["qe"] = ((q_ref[0, rows, :] * scale) * jnp.exp(cum)).astype(BF16)
            ch["ke"] = (k * jnp.exp(-cum)).astype(BF16)
            ch["kdec"] = (k * jnp.exp(cl - cum)).astype(BF16)
            ch["dec"] = jnp.exp(cl)
            ch["vb"] = v_ref[0, rows, :].astype(BF16)
        for ch in chains:
            ch["att"] = jnp.where(tri[ch["d"]], _dot(ch["qe"], ch["ke"], _NT), 0.0).astype(BF16)
        for ch in chains:
            ch["ds"] = _dot(ch["vb"], ch["kdec"], _TN)
            ch["o"] = _dot(ch["att"], ch["vb"])
        for ch in chains:
            d = ch["d"]
            ch["st_in"] = st[d].astype(BF16)
            st[d] = st[d] * ch["dec"] + ch["ds"]
        for ch in chains:
            rows = ch["rows"]
            acc_ref[rows, :] = acc_ref[rows, :] + (ch["o"] + _dot(ch["qe"], ch["st_in"], _NT))
        return tuple(st)

    st = lax.fori_loop(0, n_chunks // unroll, body, st0)
    if want_sf:
        for d in range(2):
            sf_ref[0, d, 0] = st[d].T

    gn = gn_ref[...]

    def epilogue(i, carry):
        rows = pl.ds(pl.multiple_of(i * rb, rb), rb)
        o = acc_ref[rows, :]
        on = o * lax.rsqrt(jnp.mean(o * o, axis=-1, keepdims=True) + NORM_EPS) * gn
        g = gg_ref[0, rows, :]
        o_ref[0, rows, :] = (on * (g * _sigmoid(g))).astype(o_ref.dtype)
        return carry

    lax.fori_loop(0, seq // rb, epilogue, 0)


def _gla(proj, a2p, ab, gnorm, s0, want_sf, heads, dk, dv, col_l1):
    b, t, _ = proj.shape
    has_s0 = s0 is not None
    kq = heads * dk // dk
    in_specs = [
        pl.BlockSpec((1, t, dk), lambda bi, h: (bi, 0, h)),
        pl.BlockSpec((1, t, dk), lambda bi, h: (bi, 0, heads + h)),
        pl.BlockSpec((1, t, dv), lambda bi, h: (bi, 0, (2 * heads * dk) // dv + h)),
        pl.BlockSpec((1, t, dv), lambda bi, h: (bi, 0, (2 * heads * dk) // dv + heads + h)),
        pl.BlockSpec((1, t, 2 * LANES), lambda bi, h: (bi, 0, col_l1 // (2 * LANES))),
        pl.BlockSpec((2, LANES, dk), lambda bi, h: (0, 0, h)),
        pl.BlockSpec((2, 1, dk), lambda bi, h: (0, 0, h)),
        pl.BlockSpec((1, dv), lambda bi, h: (0, 0)),
    ]
    args = [proj, proj, proj, proj, proj, a2p, ab, gnorm]
    if has_s0:
        in_specs.append(pl.BlockSpec((1, 2, 1, dk, dv), lambda bi, h: (bi, 0, h, 0, 0)))
        args.append(s0)
    out_specs = [pl.BlockSpec((1, t, dv), lambda bi, h: (bi, 0, h))]
    out_shape = [jax.ShapeDtypeStruct((b, t, heads * dv), BF16)]
    if want_sf:
        out_specs.append(pl.BlockSpec((1, 2, 1, dk, dv), lambda bi, h: (bi, 0, h, 0, 0)))
        out_shape.append(jax.ShapeDtypeStruct((b, 2, heads, dk, dv), F32))
    del kq
    res = pl.pallas_call(
        functools.partial(_gla_kernel, seq=t, chunk=GLA_CHUNK, dk=dk, has_s0=has_s0, want_sf=want_sf),
        grid=(b, heads),
        in_specs=in_specs,
        out_specs=out_specs,
        out_shape=out_shape,
        scratch_shapes=[pltpu.VMEM((t, dv), F32)],
        compiler_params=_cparams(2),
        name="gla_chunked",
    )(*args)
    return res if want_sf else (res[0], None)


def _rwkv_kernel(*refs, seq, chunk, want_sf):
    (r_ref, k_ref, v_ref, l1_ref, l1g_ref, w2_ref, a2_ref, g2_ref, w0_ref, a0_ref, mu_ref,
     kkw_ref, ka_ref, rk_ref, lng_ref, lnb_ref, s0_ref) = refs[:17]
    pos = 17
    o_ref = refs[pos]
    pos += 1
    sf_ref = None
    if want_sf:
        sf_ref = refs[pos]
        pos += 1
    rkv_s, rt_s, ys_s, g_s, ds_s, st_s = refs[pos:pos + 6]

    c = chunk
    c2 = 2 * chunk
    n_chunks = seq // c
    hn = RWKV_N

    lane = lax.broadcasted_iota(jnp.int32, (1, LANES), 1)
    head0 = lane < hn
    rowc = lax.broadcasted_iota(jnp.int32, (c, 1), 0)
    srow = lax.broadcasted_iota(jnp.int32, (c2, 1), 0)
    stack_mask = _div_pow2(srow, c) == _div_pow2(lane, hn)
    r2 = lax.broadcasted_iota(jnp.int32, (c2, c2), 0)
    q2 = lax.broadcasted_iota(jnp.int32, (c2, c2), 1)
    same_head = _div_pow2(r2, c) == _div_pow2(q2, c)
    rr = jnp.where(r2 >= c, r2 - c, r2)
    qq = jnp.where(q2 >= c, q2 - c, q2)
    eye2 = jnp.where(r2 == q2, 1.0, 0.0).astype(F32)
    lr = lax.broadcasted_iota(jnp.int32, (LANES, LANES), 0)
    lc = lax.broadcasted_iota(jnp.int32, (LANES, LANES), 1)
    diag_l = lr == lc
    block_l = _div_pow2(lr, hn) == _div_pow2(lc, hn)
    crow = lax.broadcasted_iota(jnp.int32, (c, c), 0)
    ccol = lax.broadcasted_iota(jnp.int32, (c, c), 1)

    def head_sum(x):
        s0 = jnp.sum(jnp.where(head0, x, 0.0), axis=-1, keepdims=True)
        s1 = jnp.sum(jnp.where(head0, 0.0, x), axis=-1, keepdims=True)
        return jnp.where(head0, s0, s1)

    def stack(x):
        return jnp.where(stack_mask, jnp.concatenate([x, x], axis=0), 0.0)

    def unstack(xs):
        return xs[:c] + xs[c:]

    mu = mu_ref[...]
    kkw = kkw_ref[...]
    ka = ka_ref[...]

    def shifted(ref, j, start, is_first, is_last):
        x = ref[0, pl.ds(start, c), :]
        pstart = pl.multiple_of(jnp.maximum(start - 8, 0), 8)
        nstart = pl.multiple_of(jnp.minimum(start + c, seq - 8), 8)
        prow = jnp.where(is_first, 0.0, ref[0, pl.ds(pstart, 8), :][7:8])
        nrow = jnp.where(is_last, 0.0, ref[0, pl.ds(nstart, 8), :][0:1])
        xp = jnp.where(rowc == 0, prow, pltpu.roll(x, 1, 0))
        xn = jnp.where(rowc == c - 1, nrow, pltpu.roll(x, c - 1, 0))
        return x + mu[0, j] * (xp - x) + mu[1, j] * (xn - x)

    tri_bf = [jnp.where(ccol <= crow, 1.0, 0.0).astype(BF16), jnp.where(ccol >= crow, 1.0, 0.0).astype(BF16)]
    n_levels = c.bit_length() - 1

    def level_mask(lvl, d):
        s = 1 << lvl
        siblings = jnp.logical_and(_div_pow2(rr, 2 * s) == _div_pow2(qq, 2 * s), same_head)
        r_late = (_div_pow2(rr, s) & 1) == 1
        q_late = (_div_pow2(qq, s) & 1) == 1
        if d == 0:
            return jnp.logical_and(siblings, jnp.logical_and(r_late, jnp.logical_not(q_late)))
        return jnp.logical_and(siblings, jnp.logical_and(jnp.logical_not(r_late), q_late))

    level_m = [[level_mask(lvl, d) for lvl in range(n_levels)] for d in range(2)]
    strict_m = [jnp.logical_and(qq < rr, same_head), jnp.logical_and(qq > rr, same_head)]
    incl_m = [jnp.logical_and(qq <= rr, same_head), jnp.logical_and(qq >= rr, same_head)]

    def phase_a(i, carry):
        chains = []
        for u in range(unroll):
            n = i * unroll + u
            start = pl.multiple_of(n * c, c)
            rows = pl.ds(start, c)
            is_first = n == 0
            is_last = n == n_chunks - 1
            r = shifted(r_ref, 0, start, is_first, is_last)
            k = shifted(k_ref, 1, start, is_first, is_last)
            v = shifted(v_ref, 2, start, is_first, is_last)
            rkv_s[0, rows, :] = r
            rkv_s[1, rows, :] = k
            rkv_s[2, rows, :] = v
            kk = k * kkw
            kk = kk * lax.rsqrt(head_sum(kk * kk) + 1e-12)
            vs_h = stack(v).astype(BF16)
            for d in range(2):
                chains.append(dict(n=n, rows=rows, d=d, r=r, k=k, kk=kk, vs_h=vs_h))

        for ch in chains:
            d, rows = ch["d"], ch["rows"]
            l1w = l1_ref[0, rows, d * LANES:(d + 1) * LANES]
            l1a = l1_ref[0, rows, (2 + d) * LANES:(3 + d) * LANES]
            ch["dlog"] = w0_ref[d] + _dot(jnp.tanh(l1w).astype(BF16), w2_ref[d])
            ch["alog"] = a0_ref[d] + _dot(l1a.astype(BF16), a2_ref[d])
        for ch in chains:
            dlog = ch["dlog"]
            sp = jnp.maximum(-dlog, 0.0) + jnp.log(1.0 + jnp.exp(-jnp.abs(dlog)))
            ch["lw"] = -jnp.exp(-sp - 0.5)
            a = _sigmoid(ch["alog"])
            ch["kd"] = ch["k"] * (1.0 + (a - 1.0) * ka)
            ch["bb"] = a * ch["kk"]
        for ch in chains:
            ch["lcum"] = _dot_exact_lhs(tri_bf[ch["d"]], ch["lw"])
        for ch in chains:
            d, lcum = ch["d"], ch["lcum"]
            ltot = lcum[c - 1:c] if d == 0 else lcum[0:1]
            e_nl = jnp.exp(-lcum)
            e_rem = jnp.exp(ltot - lcum)
            ch["e_tot"] = jnp.exp(ltot)
            ch["rs"] = stack(ch["r"] * jnp.exp(lcum))
            ch["ks_b"] = stack(ch["kk"] * jnp.exp(lcum - ch["lw"])).astype(BF16)
            kbs_b = stack(ch["kd"] * e_nl).astype(BF16)
            bbs_b = stack(ch["bb"] * e_nl).astype(BF16)
            ch["kts_b"] = stack(ch["kd"] * e_rem).astype(BF16)
            ch["bts_b"] = stack(ch["bb"] * e_rem).astype(BF16)
            ch["lhs"] = jnp.concatenate([ch["ks_b"], ch["rs"].astype(BF16)], axis=0)
            ch["rhs"] = jnp.concatenate([kbs_b, bbs_b], axis=0)
        for ch in chains:
            ch["aa"] = _dot(ch["lhs"], ch["rhs"], _NT)
        for ch in chains:
            d, aa = ch["d"], ch["aa"]
            ch["a_kk"] = jnp.where(strict_m[d], aa[:c2, :c2], 0.0).astype(BF16)
            ch["a_rk"] = jnp.where(incl_m[d], aa[c2:, :c2], 0.0).astype(BF16)
            ch["a_rb"] = jnp.where(incl_m[d], aa[c2:, c2:], 0.0).astype(BF16)
            ch["a_kb"] = aa[:c2, c2:]
            ch["tinv"] = eye2 - jnp.where(level_m[d][0], ch["a_kb"], 0.0)
        for ch in chains:
            ch["av"] = _dot(ch["a_kk"], ch["vs_h"]).astype(BF16)
        for ch in chains:
            ch["vk"] = _dot(ch["vs_h"], ch["kts_b"], _TN)
        for lvl in range(1, n_levels):
            for ch in chains:
                ch["tb"] = ch["tinv"].astype(BF16)
                coupling = jnp.where(level_m[ch["d"]][lvl], ch["a_kb"], 0.0).astype(BF16)
                ch["e"] = _dot(coupling, ch["tb"]).astype(BF16)
            for ch in chains:
                ch["tinv"] = ch["tinv"] - _dot(ch["tb"], ch["e"])
        for ch in chains:
            wu = _dot(ch["tinv"].astype(BF16), jnp.concatenate([ch["ks_b"], ch["av"]], axis=1))
            ch["wu_b"] = wu.astype(BF16)
        for ch in chains:
            ch["ru"] = _dot(ch["a_rb"], ch["wu_b"])
            ch["y0"] = _dot(ch["a_rk"], ch["vs_h"])
        for ch in chains:
            ch["wb"] = _dot(ch["wu_b"], ch["bts_b"], _TN)
        y0_sum = {}
        for ch in chains:
            d, n, rows = ch["d"], ch["n"], ch["rows"]
            rt = ch["rs"] - ch["ru"][:, :LANES]
            y0 = ch["y0"] - ch["ru"][:, LANES:]
            gmat = jnp.where(diag_l, ch["e_tot"], 0.0) - jnp.where(block_l, ch["wb"][:LANES], 0.0)
            dmat = jnp.where(block_l, ch["vk"] - ch["wb"][LANES:], 0.0)
            rt_s[d, rows, :] = unstack(rt).astype(BF16)
            g_s[d, n] = gmat.astype(BF16)
            ds_s[d, n] = dmat
            if d == 0:
                y0_sum = unstack(y0)
            else:
                ys_s[rows, :] = y0_sum + unstack(y0)
        return carry

    unroll = 8 if n_chunks % 16 == 0 else (4 if n_chunks % 4 == 0 else (2 if n_chunks % 2 == 0 else 1))
    lax.fori_loop(0, n_chunks // unroll, phase_a, 0)

    st_s[0] = s0_ref[0, 0, 0]
    st_s[1] = s0_ref[0, 1, 0]
    lng = lng_ref[...]
    lnb = lnb_ref[...]
    rkw = rk_ref[...]
    g2 = g2_ref[...]

    def finish(rows):
        y = ys_s[rows, :]
        mean = head_sum(y) * (1.0 / hn)
        yc = y - mean
        var = head_sum(yc * yc) * (1.0 / hn)
        yn = yc * lax.rsqrt(var + RWKV_LN_EPS) * lng + lnb
        r = rkv_s[0, rows, :]
        k = rkv_s[1, rows, :]
        v = rkv_s[2, rows, :]
        bonus = head_sum(r * k * rkw) * v
        gate = _dot(_sigmoid(l1g_ref[0, rows, :]).astype(BF16), g2)
        o_ref[0, rows, :] = ((yn + bonus) * gate).astype(o_ref.dtype)

    def chunk_rows(idx):
        return pl.ds(pl.multiple_of(idx * c, c), c)

    def state_step(i):
        for d in range(2):
            idx = i if d == 0 else n_chunks - 1 - i
            rows = chunk_rows(idx)
            st_b = st_s[d].astype(BF16)
            ys_s[rows, :] = ys_s[rows, :] + _dot(rt_s[d, rows, :], st_b, _NT)
            st_s[d] = _dot(st_b, g_s[d, idx]) + ds_s[d, idx]

    def phase_b_first(i, carry):
        state_step(i)
        return carry

    def phase_b_second(i, carry):
        finish(chunk_rows(i - 1))
        finish(chunk_rows(n_chunks - i))
        state_step(i)
        return carry

    assert n_chunks % 2 == 0
    half = n_chunks // 2
    lax.fori_loop(0, half + 1, phase_b_first, 0)
    lax.fori_loop(half + 1, n_chunks, phase_b_second, 0)
    finish(chunk_rows(n_chunks - 1))
    finish(chunk_rows(0))
    if want_sf:
        sf_ref[0, 0, 0] = st_s[0]
        sf_ref[0, 1, 0] = st_s[1]


def _rwkv(proj, w2p, a2p, g2, w0, a0, mu, kkw, ka, rkw, lng, lnb, s0_bd, want_sf, col_rkv, col_l1, col_g):
    b, t, _ = proj.shape
    width = kkw.shape[-1]
    pairs = width // LANES
    c = RWKV_CHUNK
    n_chunks = t // c
    cb = col_rkv // LANES
    vec = lambda: pl.BlockSpec((1, LANES), lambda bi, h: (0, h))
    in_specs = [
        pl.BlockSpec((1, t, LANES), lambda bi, h: (bi, 0, cb + h)),
        pl.BlockSpec((1, t, LANES), lambda bi, h: (bi, 0, cb + pairs + h)),
        pl.BlockSpec((1, t, LANES), lambda bi, h: (bi, 0, cb + 2 * pairs + h)),
        pl.BlockSpec((1, t, 4 * LANES), lambda bi, h: (bi, 0, col_l1 // (4 * LANES))),
        pl.BlockSpec((1, t, 2 * LANES), lambda bi, h: (bi, 0, col_g // (2 * LANES))),
        pl.BlockSpec((2, LANES, LANES), lambda bi, h: (0, 0, h)),
        pl.BlockSpec((2, LANES, LANES), lambda bi, h: (0, 0, h)),
        pl.BlockSpec((2 * LANES, LANES), lambda bi, h: (0, h)),
        pl.BlockSpec((2, 1, LANES), lambda bi, h: (0, 0, h)),
        pl.BlockSpec((2, 1, LANES), lambda bi, h: (0, 0, h)),
        pl.BlockSpec((2, 3, 1, LANES), lambda bi, h: (0, 0, 0, h)),
        vec(), vec(), vec(), vec(), vec(),
        pl.BlockSpec((1, 2, 1, LANES, LANES), lambda bi, h: (bi, 0, h, 0, 0)),
    ]
    out_specs = [pl.BlockSpec((1, t, LANES), lambda bi, h: (bi, 0, h))]
    out_shape = [jax.ShapeDtypeStruct((b, t, width), BF16)]
    if want_sf:
        out_specs.append(pl.BlockSpec((1, 2, 1, LANES, LANES), lambda bi, h: (bi, 0, h, 0, 0)))
        out_shape.append(jax.ShapeDtypeStruct((b, 2, pairs, LANES, LANES), F32))
    res = pl.pallas_call(
        functools.partial(_rwkv_kernel, seq=t, chunk=c, want_sf=want_sf),
        grid=(b, pairs),
        in_specs=in_specs,
        out_specs=out_specs,
        out_shape=out_shape,
        scratch_shapes=[pltpu.VMEM((3, t, LANES), F32),
                        pltpu.VMEM((2, t, LANES), BF16),
                        pltpu.VMEM((t, LANES), F32),
                        pltpu.VMEM((2, n_chunks, LANES, LANES), BF16),
                        pltpu.VMEM((2, n_chunks, LANES, LANES), F32),
                        pltpu.VMEM((2, LANES, LANES), F32)],
        compiler_params=_cparams(2),
        name="rwkv7_chunked",
    )(proj, proj, proj, proj, proj, w2p, a2p, g2, w0, a0, mu, kkw, ka, rkw, lng, lnb, s0_bd)
    return res if want_sf else (res[0], None)


def _rope(x, cos, sin, first_half):
    part = jnp.where(first_half, pltpu.roll(x, x.shape[-1] - 32, 1), pltpu.roll(x, 32, 1))
    return x * cos + part * sin


def _diff_attn_kernel(*refs, rope, has_cache, lam_init, tq):
    q_ref, k_ref, v_ref, lq_ref, lk_ref, sub_ref = refs[:6]
    pos = 6
    if rope:
        cos_ref, sin_ref = refs[pos:pos + 2]
        pos += 2
    if has_cache:
        ck_ref, cv_ref = refs[pos:pos + 2]
        pos += 2
    o_ref = refs[pos]
    ks_ref, vs_ref = refs[pos + 1:pos + 3]

    t = k_ref.shape[2]
    w = k_ref.shape[3]
    dh = DIFF_DH
    i = pl.program_id(2)
    lane = lax.broadcasted_iota(jnp.int32, (1, w), 1)
    first_half = (lane & 63) < 32

    @pl.when(i == 0)
    def _():
        kb = min(t, 512)

        def fill(j, carry):
            rows = pl.ds(pl.multiple_of(j * kb, kb), kb)
            kx = k_ref[0, 0, rows, :]
            if rope:
                kx = _rope(kx, cos_ref[rows, :], sin_ref[rows, :], first_half)
            ks_ref[rows, :] = kx.astype(BF16)
            vs_ref[rows, :] = v_ref[0, 0, rows, :].astype(BF16)
            return carry

        lax.fori_loop(0, t // kb, fill, 0)
        if has_cache:
            ks_ref[t:, :] = ck_ref[0, 0, 0].astype(BF16)
            vs_ref[t:, :] = cv_ref[0, 0, 0].astype(BF16)

    q = q_ref[0, 0]
    if rope:
        qrows = pl.ds(pl.multiple_of(i * tq, tq), tq)
        q = _rope(q, cos_ref[qrows, :], sin_ref[qrows, :], first_half)
    qb = q.astype(BF16)
    lq = lq_ref[...]
    lk = lk_ref[...]
    lam = (jnp.exp(jnp.sum(lq[0:1] * lk[0:1], axis=-1, keepdims=True))
           - jnp.exp(jnp.sum(lq[1:2] * lk[1:2], axis=-1, keepdims=True)) + lam_init)
    kall = ks_ref[...]
    vall = vs_ref[...]
    exp2_scale = dh ** -0.5 * math.log2(math.e)
    sub = min(tq, 128)
    scores = [[_dot(qb[r:r + sub, m * dh:(m + 1) * dh], kall[:, m * dh:(m + 1) * dh], _NT) for m in range(2)]
              for r in range(0, tq, sub)]
    rows_out = []
    for pair in scores:
        es, invs = [], []
        for s in pair:
            e = jnp.exp2((s - jnp.max(s, axis=-1, keepdims=True)) * exp2_scale)
            es.append(e)
            invs.append(1.0 / jnp.sum(e, axis=-1, keepdims=True))
        p = es[0] * invs[0] - es[1] * (lam * invs[1])
        rows_out.append(_dot(p.astype(BF16), vall))
    o = jnp.concatenate(rows_out, axis=0) if len(rows_out) > 1 else rows_out[0]
    on = o * lax.rsqrt(jnp.mean(o * o, axis=-1, keepdims=True) + SUBLN_EPS) * sub_ref[...]
    o_ref[0] = (on * (1.0 - lam_init)).astype(o_ref.dtype)


def _diff_attention(q, k, v, lq, lk, subln, lam_init, rope_tabs, cache_k, cache_v, tq=512):
    b, heads, t, w = q.shape
    tq = _pick(t, tq)
    rope = rope_tabs is not None
    has_cache = cache_k is not None
    tk = t + (cache_k.shape[3] if has_cache else 0)
    in_specs = [
        pl.BlockSpec((1, 1, tq, w), lambda bi, h, i: (bi, h, i, 0)),
        pl.BlockSpec((1, 1, t, w), lambda bi, h, i: (bi, h, 0, 0)),
        pl.BlockSpec((1, 1, t, w), lambda bi, h, i: (bi, h, 0, 0)),
        pl.BlockSpec((2, DIFF_DH), lambda bi, h, i: (0, 0)),
        pl.BlockSpec((2, DIFF_DH), lambda bi, h, i: (0, 0)),
        pl.BlockSpec((1, w), lambda bi, h, i: (0, 0)),
    ]
    args = [q, k, v, lq, lk, subln.reshape(1, w)]
    if rope:
        in_specs += [pl.BlockSpec((t, w), lambda bi, h, i: (0, 0))] * 2
        args += list(rope_tabs)
    if has_cache:
        pc = cache_k.shape[3]
        in_specs += [pl.BlockSpec((1, 1, 1, pc, w), lambda bi, h, i: (bi, 0, h, 0, 0))] * 2
        args += [cache_k, cache_v]
    return pl.pallas_call(
        functools.partial(_diff_attn_kernel, rope=rope, has_cache=has_cache, lam_init=lam_init, tq=tq),
        grid=(b, heads, t // tq),
        in_specs=in_specs,
        out_specs=pl.BlockSpec((1, tq, w), lambda bi, h, i: (bi, i, h)),
        out_shape=jax.ShapeDtypeStruct((b, t, heads * w), BF16),
        scratch_shapes=[pltpu.VMEM((tk, w), BF16), pltpu.VMEM((tk, w), BF16)],
        compiler_params=_cparams(3),
        name="diff_attention",
    )(*args)


def _rope_tables(t, w):
    nf = DIFF_DH // 4
    rows = t // GRID_W
    row = jnp.repeat(jnp.arange(rows, dtype=F32), GRID_W)
    col = jnp.tile(jnp.arange(GRID_W, dtype=F32), rows)
    inv = jnp.power(ROPE_BASE, -jnp.arange(nf, dtype=F32) / nf)
    ang_r = row[:, None] * inv
    ang_c = col[:, None] * inv
    cos = jnp.concatenate([jnp.cos(ang_r), jnp.cos(ang_r), jnp.cos(ang_c), jnp.cos(ang_c)], axis=-1)
    sin = jnp.concatenate([-jnp.sin(ang_r), jnp.sin(ang_r), -jnp.sin(ang_c), jnp.sin(ang_c)], axis=-1)
    reps = w // DIFF_DH
    return jnp.tile(cos, (1, reps)), jnp.tile(sin, (1, reps))


def _pad_rows(w, rows):
    return jnp.pad(w, ((0, 0),) * (w.ndim - 2) + ((0, rows - w.shape[-2]), (0, 0)))


def _pad_cols(w, cols):
    return jnp.pad(w, ((0, 0),) * (w.ndim - 1) + ((0, cols - w.shape[-1]),))


def kernel(x_prompt, x_sample, state_gla, state_rwkv, cache_k, cache_v, c, c_ctx, ada_w, ada_b, norm_mix, norm_ffn, norm_final, ffn_w1, ffn_w3, ffn_w2, even_w_in, even_w_out, gla_a1, gla_a2, gla_ab, gla_norm, rw_mu, rw_w0, rw_w1, rw_w2, rw_a0, rw_a1, rw_a2, rw_g1, rw_g2, rw_kk, rw_ka, rw_rk, rw_ln_g, rw_ln_b, odd_w_in, odd_w_out, diff_lq, diff_lk, diff_subln):
    depth, d, _ = ada_w.shape
    dec_b = x_sample.shape[0]
    gla_heads, gla_dk, gla_dv = state_gla.shape[3:]
    rw_heads = state_rwkv.shape[3]
    rw_w = rw_heads * RWKV_N
    pairs = rw_w // LANES
    diff_heads, _, diff_w = cache_k.shape[2:]
    gla_qk = gla_heads * gla_dk
    gla_vw = gla_heads * gla_dv
    even_in = even_w_in.shape[-1]

    nb = -(-(1 + dec_b) // 8) * 8
    cond = jnp.zeros((nb, d), F32).at[0].set(c_ctx).at[1:1 + dec_b].set(c)
    mod = _adaln(cond, ada_w, ada_b)

    def mods(layer, prompt):
        m = mod[layer, 0:1] if prompt else mod[layer, 1:1 + dec_b]
        return [m[:, None, k * d:(k + 1) * d] for k in range(6)]

    def even_weights(e):
        lora1 = jnp.concatenate(
            [_pad_cols(rw_w1[e, 0], LANES), _pad_cols(rw_w1[e, 1], LANES),
             _pad_cols(rw_a1[e, 0], LANES), _pad_cols(rw_a1[e, 1], LANES),
             rw_g1[e],
             _pad_cols(gla_a1[e, 0], LANES), _pad_cols(gla_a1[e, 1], LANES)], axis=-1)
        w_cat = jnp.concatenate([even_w_in[e], lora1], axis=-1).astype(BF16)
        return dict(
            w_cat=w_cat,
            w_out=even_w_out[e].astype(BF16),
            gla_a2=_pad_rows(gla_a2[e], LANES),
            gla_ab=gla_ab[e].reshape(2, 1, gla_qk),
            gla_norm=gla_norm[e].reshape(1, gla_dv),
            w2=_pad_rows(rw_w2[e], LANES).astype(BF16),
            a2=_pad_rows(rw_a2[e], LANES).astype(BF16),
            g2=rw_g2[e].astype(BF16),
            w0=rw_w0[e].reshape(2, 1, rw_w),
            a0=rw_a0[e].reshape(2, 1, rw_w),
            mu=rw_mu[e].reshape(2, 3, 1, rw_w),
            kk=rw_kk[e].reshape(1, rw_w), ka=rw_ka[e].reshape(1, rw_w), rk=rw_rk[e].reshape(1, rw_w),
            lng=rw_ln_g[e].reshape(1, rw_w), lnb=rw_ln_b[e].reshape(1, rw_w),
        )

    col_rkv = 2 * gla_qk + 2 * gla_vw
    col_l1 = even_in
    col_g = even_in + 4 * LANES
    col_gl = even_in + 6 * LANES

    def block_diag_state(s):
        bsz = s.shape[0]
        s = s.reshape(bsz, 2, pairs, 2, RWKV_N, RWKV_N)
        z = jnp.zeros_like(s[:, :, :, 0])
        top = jnp.concatenate([s[:, :, :, 0], z], axis=-1)
        bot = jnp.concatenate([z, s[:, :, :, 1]], axis=-1)
        return jnp.concatenate([top, bot], axis=-2)

    def diag_blocks(sbd):
        bsz = sbd.shape[0]
        h0 = sbd[:, :, :, :RWKV_N, :RWKV_N]
        h1 = sbd[:, :, :, RWKV_N:, RWKV_N:]
        return jnp.stack([h0, h1], axis=3).reshape(bsz, 2, rw_heads, RWKV_N, RWKV_N)

    def even_mixer(h, ew, s_gla, s_rwkv, want_states):
        proj = _matmul(h, ew["w_cat"], tn=1024, name="even_in_proj")
        o_gla, sg = _gla(proj, ew["gla_a2"], ew["gla_ab"], ew["gla_norm"], s_gla, want_states,
                         gla_heads, gla_dk, gla_dv, col_gl)
        if s_rwkv is None:
            s_bd = jnp.zeros((h.shape[0], 2, pairs, LANES, LANES), F32)
        else:
            s_bd = block_diag_state(s_rwkv)
        y_rw, sr = _rwkv(proj, ew["w2"], ew["a2"], ew["g2"], ew["w0"], ew["a0"], ew["mu"], ew["kk"], ew["ka"],
                         ew["rk"], ew["lng"], ew["lnb"], s_bd, want_states, col_rkv, col_l1, col_g)
        mix = [o_gla, y_rw]
        return mix, ew["w_out"], sg, (diag_blocks(sr) if want_states else None)

    def run_group(x, prompt):
        outs = {}
        for layer in range(depth):
            sh1, sc1, g1, sh2, sc2, g2 = mods(layer, prompt)
            h = _norm_mod(x, norm_mix[layer], sc1, sh1)
            if layer % 2 == 0:
                e = layer // 2
                ew = even_weights(e)
                if prompt:
                    mix, w_out, sg, sr = even_mixer(h, ew, None, None, True)
                    outs.setdefault("gla", []).append(sg)
                    outs.setdefault("rwkv", []).append(sr)
                else:
                    mix, w_out, _, _ = even_mixer(h, ew, state_gla[:, e], state_rwkv[:, e], False)
            else:
                o = layer // 2
                lam_init = 0.8 - 0.6 * math.exp(-0.3 * layer)
                q, k, v = _matmul_heads(h, odd_w_in[o].astype(BF16), diff_heads, diff_w)
                if prompt:
                    outs.setdefault("k", []).append(k)
                    outs.setdefault("v", []).append(v)
                    mix = _diff_attention(q, k, v, diff_lq[o], diff_lk[o], diff_subln[o], lam_init,
                                          None, None, None)
                else:
                    mix = _diff_attention(q, k, v, diff_lq[o], diff_lk[o], diff_subln[o], lam_init,
                                          _rope_tables(x.shape[1], diff_w), cache_k[:, o:o + 1], cache_v[:, o:o + 1])
                w_out = odd_w_out[o].astype(BF16)
            x = _matmul_residual(mix, w_out, x, g1, tm=512, tn=2048, name="mix_out_proj")
            h = _norm_mod(x, norm_ffn[layer], sc2, sh2)
            u = _matmul_swiglu(h, ffn_w1[layer].astype(BF16), ffn_w3[layer].astype(BF16))
            x = _matmul_residual(u, ffn_w2[layer].astype(BF16), x, g2, name="ffn_down_proj")
        return _final_norm(x, norm_final), outs

    y_prompt, po = run_group(x_prompt, True)
    y_sample, _ = run_group(x_sample, False)
    return (y_prompt, y_sample,
            jnp.stack(po["gla"], axis=1), jnp.stack(po["rwkv"], axis=1),
            jnp.stack(po["k"], axis=1), jnp.stack(po["v"], axis=1))
```

```python
import functools
import math

import jax
import jax.numpy as jnp
from jax import lax
from jax.experimental import pallas as pl
from jax.experimental.pallas import tpu as pltpu

F32 = jnp.float32
BF16 = jnp.bfloat16

V7X_VMEM_LIMIT_BYTES = 56 * 1024 * 1024
LANES = 128

NORM_EPS = 1e-6
GLA_GATE_NORM = 16.0
GLA_CHUNK = 64
RWKV_CHUNK = 64
RWKV_N = 64
RWKV_LN_EPS = 64e-5
DIFF_DH = 128
GRID_W = 64
ROPE_BASE = 10000.0
SUBLN_EPS = 1e-5


def _cparams(n_axes):
    return pltpu.CompilerParams(dimension_semantics=("arbitrary",) * n_axes,
                                vmem_limit_bytes=V7X_VMEM_LIMIT_BYTES)


def _dot(a, b, dims=(((1,), (0,)), ((), ()))):
    return lax.dot_general(a, b, dims, preferred_element_type=F32)


_NN = (((1,), (0,)), ((), ()))
_NT = (((1,), (1,)), ((), ()))
_TN = (((0,), (0,)), ((), ()))


def _split3(x):
    h = x.astype(BF16)
    r = x - h.astype(F32)
    m = r.astype(BF16)
    l = (r - m.astype(F32)).astype(BF16)
    return h, m, l


def _split2(x):
    h = x.astype(BF16)
    m = (x - h.astype(F32)).astype(BF16)
    return h, m


def _dot_hp(a, b, dims=_NN):
    ah, al = _split2(a)
    bh, bl = _split2(b)
    return _dot(ah, bh, dims) + (_dot(ah, bl, dims) + _dot(al, bh, dims))


def _dot_exact_lhs(a_bf, b, dims=_NN):
    h, m, l = _split3(b)
    return _dot(a_bf, h, dims) + (_dot(a_bf, m, dims) + _dot(a_bf, l, dims))


def _div_pow2(x, n):
    assert n & (n - 1) == 0
    return lax.shift_right_logical(x, n.bit_length() - 1)


def _sigmoid(x):
    return 1.0 / (1.0 + jnp.exp(-x))


def _log_sigmoid(z):
    return jnp.minimum(z, 0.0) - jnp.log(1.0 + jnp.exp(-jnp.abs(z)))


def _adaln_kernel(c_ref, w_ref, b_ref, o_ref):
    cs = c_ref[...]
    s = cs * _sigmoid(cs)
    o_ref[0] = _dot(s.astype(BF16), w_ref[0].astype(BF16)) + b_ref[0]


def _adaln(cond, ada_w, ada_b, tn=1024):
    depth, d, n = ada_w.shape
    nb = cond.shape[0]
    return pl.pallas_call(
        _adaln_kernel,
        grid=(depth, n // tn),
        in_specs=[pl.BlockSpec((nb, d), lambda l, j: (0, 0)),
                  pl.BlockSpec((1, d, tn), lambda l, j: (l, 0, j)),
                  pl.BlockSpec((1, 1, tn), lambda l, j: (l, 0, j))],
        out_specs=pl.BlockSpec((1, nb, tn), lambda l, j: (l, 0, j)),
        out_shape=jax.ShapeDtypeStruct((depth, nb, n), F32),
        compiler_params=_cparams(2),
        name="adaln_mod",
    )(cond, ada_w, ada_b.reshape(depth, 1, n))


def _norm_mod_kernel(x_ref, g_ref, sc_ref, sh_ref, o_ref):
    x = x_ref[0]
    xn = x * lax.rsqrt(jnp.mean(x * x, axis=-1, keepdims=True) + NORM_EPS)
    o_ref[0] = ((xn * g_ref[...]) * (1.0 + sc_ref[0]) + sh_ref[0]).astype(o_ref.dtype)


def _norm_mod(x, g, sc, sh, tt=1024):
    b, t, d = x.shape
    tt = _pick(t, tt)
    per_batch = sc.shape[0] > 1
    mod_map = (lambda i, j: (i, 0, 0)) if per_batch else (lambda i, j: (0, 0, 0))
    return pl.pallas_call(
        _norm_mod_kernel,
        grid=(b, t // tt),
        in_specs=[pl.BlockSpec((1, tt, d), lambda i, j: (i, j, 0)),
                  pl.BlockSpec((1, d), lambda i, j: (0, 0)),
                  pl.BlockSpec((1, 1, d), mod_map),
                  pl.BlockSpec((1, 1, d), mod_map)],
        out_specs=pl.BlockSpec((1, tt, d), lambda i, j: (i, j, 0)),
        out_shape=jax.ShapeDtypeStruct((b, t, d), BF16),
        compiler_params=_cparams(2),
        name="norm_mod",
    )(x, g.reshape(1, d), sc, sh)


def _final_norm_kernel(x_ref, g_ref, o_ref):
    x = x_ref[0]
    xn = x * lax.rsqrt(jnp.mean(x * x, axis=-1, keepdims=True) + NORM_EPS)
    o_ref[0] = xn * g_ref[...]


def _final_norm(x, g, tt=1024):
    b, t, d = x.shape
    tt = _pick(t, tt)
    return pl.pallas_call(
        _final_norm_kernel,
        grid=(b, t // tt),
        in_specs=[pl.BlockSpec((1, tt, d), lambda i, j: (i, j, 0)),
                  pl.BlockSpec((1, d), lambda i, j: (0, 0))],
        out_specs=pl.BlockSpec((1, tt, d), lambda i, j: (i, j, 0)),
        out_shape=jax.ShapeDtypeStruct((b, t, d), F32),
        compiler_params=_cparams(2),
        name="final_norm",
    )(x, g.reshape(1, d))


def _mm_kernel(x_ref, w_ref, o_ref):
    o_ref[...] = _dot(x_ref[...], w_ref[...]).astype(o_ref.dtype)


def _pick(n, pref):
    t = min(n, pref)
    while n % t:
        t //= 2
    return t


def _matmul(x, w, out_dtype=F32, tm=1024, tn=512, name="matmul"):
    b, t, k = x.shape
    n = w.shape[1]
    m = b * t
    tm, tn = _pick(m, tm), _pick(n, tn)
    out = pl.pallas_call(
        _mm_kernel,
        grid=(m // tm, n // tn),
        in_specs=[pl.BlockSpec((tm, k), lambda i, j: (i, 0)),
                  pl.BlockSpec((k, tn), lambda i, j: (0, j))],
        out_specs=pl.BlockSpec((tm, tn), lambda i, j: (i, j)),
        out_shape=jax.ShapeDtypeStruct((m, n), out_dtype),
        compiler_params=_cparams(2),
        name=name,
    )(x.reshape(m, k), w)
    return out.reshape(b, t, n)


def _mm_res_kernel(*refs, n_in):
    r_ref, g_ref, o_ref = refs[2 * n_in:]
    acc = _dot(refs[0][...], refs[n_in][...])
    for i in range(1, n_in):
        acc = acc + _dot(refs[i][...], refs[n_in + i][...])
    o_ref[...] = r_ref[...] + g_ref[0] * acc


def _matmul_residual(xs, w, res, gate, tm=1024, tn=512, name="matmul_residual"):
    xs = xs if isinstance(xs, (list, tuple)) else [xs]
    b, t, k = xs[0].shape
    n_in = len(xs)
    n = w.shape[1]
    m = b * t
    per_batch = gate.shape[0] > 1
    tm = _pick(t if per_batch else m, tm)
    tn = _pick(n, tn)
    gate_map = (lambda i, j: ((i * tm) // t, 0, j)) if per_batch else (lambda i, j: (0, 0, j))
    out = pl.pallas_call(
        functools.partial(_mm_res_kernel, n_in=n_in),
        grid=(m // tm, n // tn),
        in_specs=([pl.BlockSpec((tm, k), lambda i, j: (i, 0))] * n_in
                  + [pl.BlockSpec((k, tn), functools.partial(lambda i, j, p: (p, j), p=p)) for p in range(n_in)]
                  + [pl.BlockSpec((tm, tn), lambda i, j: (i, j)),
                     pl.BlockSpec((1, 1, tn), gate_map)]),
        out_specs=pl.BlockSpec((tm, tn), lambda i, j: (i, j)),
        out_shape=jax.ShapeDtypeStruct((m, n), F32),
        compiler_params=_cparams(2),
        name=name,
    )(*[x.reshape(m, k) for x in xs], *([w] * n_in), res.reshape(m, n), gate)
    return out.reshape(b, t, n)


def _mm_swiglu_kernel(x_ref, w1_ref, w3_ref, o_ref):
    x = x_ref[...]
    a = _dot(x, w1_ref[...])
    o_ref[...] = ((a * _sigmoid(a)) * _dot(x, w3_ref[...])).astype(o_ref.dtype)


def _matmul_swiglu(x, w1, w3, tm=1024, tn=512):
    b, t, k = x.shape
    n = w1.shape[1]
    m = b * t
    tm, tn = _pick(m, tm), _pick(n, tn)
    out = pl.pallas_call(
        _mm_swiglu_kernel,
        grid=(m // tm, n // tn),
        in_specs=[pl.BlockSpec((tm, k), lambda i, j: (i, 0)),
                  pl.BlockSpec((k, tn), lambda i, j: (0, j)),
                  pl.BlockSpec((k, tn), lambda i, j: (0, j))],
        out_specs=pl.BlockSpec((tm, tn), lambda i, j: (i, j)),
        out_shape=jax.ShapeDtypeStruct((m, n), BF16),
        compiler_params=_cparams(2),
        name="swiglu_up",
    )(x.reshape(m, k), w1, w3)
    return out.reshape(b, t, n)


def _mm_heads_kernel(x_ref, wq_ref, wk_ref, wv_ref, q_ref, k_ref, v_ref, *, nbt, rows, hb, hw):
    x = x_ref[...]
    for w_ref, o_ref in ((wq_ref, q_ref), (wk_ref, k_ref), (wv_ref, v_ref)):
        r = _dot(x, w_ref[...])
        for bb in range(nbt):
            for i in range(hb):
                o_ref[bb, i] = r[bb * rows:(bb + 1) * rows, i * hw:(i + 1) * hw]


def _matmul_heads(x, w, heads, hw, hb=2, tm=1024):
    b, t, k = x.shape
    m = b * t
    tm = _pick(m, tm)
    if tm >= t:
        nbt, rows = tm // t, t
        out_map = lambda i, j: (i, j, 0, 0)
    else:
        nbt, rows = 1, tm
        per_seq = t // tm
        out_map = lambda i, j: (i // per_seq, j, i % per_seq, 0)
    hblocks = heads // hb
    out_spec = pl.BlockSpec((nbt, hb, rows, hw), out_map)
    return pl.pallas_call(
        functools.partial(_mm_heads_kernel, nbt=nbt, rows=rows, hb=hb, hw=hw),
        grid=(m // tm, hblocks),
        in_specs=[pl.BlockSpec((tm, k), lambda i, j: (i, 0)),
                  pl.BlockSpec((k, hb * hw), lambda i, j: (0, j)),
                  pl.BlockSpec((k, hb * hw), lambda i, j: (0, hblocks + j)),
                  pl.BlockSpec((k, hb * hw), lambda i, j: (0, 2 * hblocks + j))],
        out_specs=[out_spec] * 3,
        out_shape=[jax.ShapeDtypeStruct((b, heads, t, hw), F32)] * 3,
        compiler_params=_cparams(2),
        name="qkv_heads",
    )(x.reshape(m, k), w, w, w)


def _gla_kernel(*refs, seq, chunk, dk, has_s0, want_sf):
    q_ref, k_ref, v_ref, gg_ref, l1_ref, a2_ref, ab_ref, gn_ref = refs[:8]
    pos = 8
    s0_ref = None
    if has_s0:
        s0_ref = refs[pos]
        pos += 1
    o_ref = refs[pos]
    pos += 1
    sf_ref = None
    if want_sf:
        sf_ref = refs[pos]
        pos += 1
    acc_ref = refs[pos]

    n_chunks = seq // chunk
    dv = v_ref.shape[-1]
    scale = dk ** -0.5
    row = lax.broadcasted_iota(jnp.int32, (chunk, chunk), 0)
    col = lax.broadcasted_iota(jnp.int32, (chunk, chunk), 1)

    tri = [col <= row, col >= row]
    tri_bf = [jnp.where(t, 1.0, 0.0).astype(BF16) for t in tri]
    a2 = [a2_ref[d].astype(BF16) for d in range(2)]
    ab = [ab_ref[d] for d in range(2)]
    if has_s0:
        st0 = tuple(s0_ref[0, d, 0].T for d in range(2))
    else:
        st0 = tuple(jnp.zeros((dv, dk), F32) for d in range(2))
    unroll = 8 if n_chunks % 8 == 0 else (4 if n_chunks % 4 == 0 else (2 if n_chunks % 2 == 0 else 1))
    rb = min(seq, 256)

    def zero(i, carry):
        acc_ref[pl.ds(pl.multiple_of(i * rb, rb), rb), :] = jnp.zeros((rb, dv), F32)
        return carry

    lax.fori_loop(0, seq // rb, zero, 0)

    def body(i, carry):
        st = list(carry)
        chains = []
        for u in range(unroll):
            for d in range(2):
                n = i * unroll + u
                idx = n if d == 0 else n_chunks - 1 - n
                chains.append(dict(d=d, rows=pl.ds(pl.multiple_of(idx * chunk, chunk), chunk)))
        for ch in chains:
            d, rows = ch["d"], ch["rows"]
            l1 = l1_ref[0, rows, d * LANES:(d + 1) * LANES]
            ch["z"] = _dot(l1.astype(BF16), a2[d]) + ab[d]
        for ch in chains:
            ch["la"] = _log_sigmoid(ch["z"]) / GLA_GATE_NORM
        for ch in chains:
            ch["cum"] = _dot_exact_lhs(tri_bf[ch["d"]], ch["la"])
        for ch in chains:
            d, rows, cum = ch["d"], ch["rows"], ch["cum"]
            cl = cum[chunk - 1:chunk] if d == 0 else cum[0:1]
            k = k_ref[0, rows, :]
            ch["qe"] = ((q_ref[0, rows, :] * scale) * jnp.exp(cum)).astype(BF16)
            ch["ke"] = (k * jnp.exp(-cum)).astype(BF16)
            ch["kdec"] = (k * jnp.exp(cl - cum)).astype(BF16)
            ch["dec"] = jnp.exp(cl)
            ch["vb"] = v_ref[0, rows, :].astype(BF16)
        for ch in chains:
            ch["att"] = jnp.where(tri[ch["d"]], _dot(ch["qe"], ch["ke"], _NT), 0.0).astype(BF16)
        for ch in chains:
            ch["ds"] = _dot(ch["vb"], ch["kdec"], _TN)
            ch["o"] = _dot(ch["att"], ch["vb"])
        for ch in chains:
            d = ch["d"]
            ch["st_in"] = st[d].astype(BF16)
            st[d] = st[d] * ch["dec"] + ch["ds"]
        for ch in chains:
            rows = ch["rows"]
            acc_ref[rows, :] = acc_ref[rows, :] + (ch["o"] + _dot(ch["qe"], ch["st_in"], _NT))
        return tuple(st)

    st = lax.fori_loop(0, n_chunks // unroll, body, st0)
    if want_sf:
        for d in range(2):
            sf_ref[0, d, 0] = st[d].T

    gn = gn_ref[...]

    def epilogue(i, carry):
        rows = pl.ds(pl.multiple_of(i * rb, rb), rb)
        o = acc_ref[rows, :]
        on = o * lax.rsqrt(jnp.mean(o * o, axis=-1, keepdims=True) + NORM_EPS) * gn
        g = gg_ref[0, rows, :]
        o_ref[0, rows, :] = (on * (g * _sigmoid(g))).astype(o_ref.dtype)
        return carry

    lax.fori_loop(0, seq // rb, epilogue, 0)


def _gla(proj, a2p, ab, gnorm, s0, want_sf, heads, dk, dv, col_l1):
    b, t, _ = proj.shape
    has_s0 = s0 is not None
    kq = heads * dk // dk
    in_specs = [
        pl.BlockSpec((1, t, dk), lambda bi, h: (bi, 0, h)),
        pl.BlockSpec((1, t, dk), lambda bi, h: (bi, 0, heads + h)),
        pl.BlockSpec((1, t, dv), lambda bi, h: (bi, 0, (2 * heads * dk) // dv + h)),
        pl.BlockSpec((1, t, dv), lambda bi, h: (bi, 0, (2 * heads * dk) // dv + heads + h)),
        pl.BlockSpec((1, t, 2 * LANES), lambda bi, h: (bi, 0, col_l1 // (2 * LANES))),
        pl.BlockSpec((2, LANES, dk), lambda bi, h: (0, 0, h)),
        pl.BlockSpec((2, 1, dk), lambda bi, h: (0, 0, h)),
        pl.BlockSpec((1, dv), lambda bi, h: (0, 0)),
    ]
    args = [proj, proj, proj, proj, proj, a2p, ab, gnorm]
    if has_s0:
        in_specs.append(pl.BlockSpec((1, 2, 1, dk, dv), lambda bi, h: (bi, 0, h, 0, 0)))
        args.append(s0)
    out_specs = [pl.BlockSpec((1, t, dv), lambda bi, h: (bi, 0, h))]
    out_shape = [jax.ShapeDtypeStruct((b, t, heads * dv), BF16)]
    if want_sf:
        out_specs.append(pl.BlockSpec((1, 2, 1, dk, dv), lambda bi, h: (bi, 0, h, 0, 0)))
        out_shape.append(jax.ShapeDtypeStruct((b, 2, heads, dk, dv), F32))
    del kq
    res = pl.pallas_call(
        functools.partial(_gla_kernel, seq=t, chunk=GLA_CHUNK, dk=dk, has_s0=has_s0, want_sf=want_sf),
        grid=(b, heads),
        in_specs=in_specs,
        out_specs=out_specs,
        out_shape=out_shape,
        scratch_shapes=[pltpu.VMEM((t, dv), F32)],
        compiler_params=_cparams(2),
        name="gla_chunked",
    )(*args)
    return res if want_sf else (res[0], None)


def _rwkv_kernel(*refs, seq, chunk, nseq, want_sf):
    (r_ref, k_ref, v_ref, l1_ref, l1g_ref, w2_ref, a2_ref, g2_ref, w0_ref, a0_ref, mu_ref,
     kkw_ref, ka_ref, rk_ref, lng_ref, lnb_ref, s0_ref) = refs[:17]
    pos = 17
    o_ref = refs[pos]
    pos += 1
    sf_ref = None
    if want_sf:
        sf_ref = refs[pos]
        pos += 1
    rkv_s, rt_s, ys_s, g_s, ds_s, st_s = refs[pos:pos + 6]

    c = chunk
    c2 = 2 * chunk
    n_chunks = seq // c
    hn = RWKV_N

    lane = lax.broadcasted_iota(jnp.int32, (1, LANES), 1)
    head0 = lane < hn
    rowc = lax.broadcasted_iota(jnp.int32, (c, 1), 0)
    srow = lax.broadcasted_iota(jnp.int32, (c2, 1), 0)
    stack_mask = _div_pow2(srow, c) == _div_pow2(lane, hn)
    r2 = lax.broadcasted_iota(jnp.int32, (c2, c2), 0)
    q2 = lax.broadcasted_iota(jnp.int32, (c2, c2), 1)
    same_head = _div_pow2(r2, c) == _div_pow2(q2, c)
    rr = jnp.where(r2 >= c, r2 - c, r2)
    qq = jnp.where(q2 >= c, q2 - c, q2)
    eye2 = jnp.where(r2 == q2, 1.0, 0.0).astype(F32)
    lr = lax.broadcasted_iota(jnp.int32, (LANES, LANES), 0)
    lc = lax.broadcasted_iota(jnp.int32, (LANES, LANES), 1)
    diag_l = lr == lc
    block_l = _div_pow2(lr, hn) == _div_pow2(lc, hn)
    crow = lax.broadcasted_iota(jnp.int32, (c, c), 0)
    ccol = lax.broadcasted_iota(jnp.int32, (c, c), 1)

    def head_sum(x):
        s0 = jnp.sum(jnp.where(head0, x, 0.0), axis=-1, keepdims=True)
        s1 = jnp.sum(jnp.where(head0, 0.0, x), axis=-1, keepdims=True)
        return jnp.where(head0, s0, s1)

    def stack(x):
        return jnp.where(stack_mask, jnp.concatenate([x, x], axis=0), 0.0)

    def unstack(xs):
        return xs[:c] + xs[c:]

    mu = mu_ref[...]
    kkw = kkw_ref[...]
    ka = ka_ref[...]

    def shifted(ref, sq, j, start, is_first, is_last):
        x = ref[sq, pl.ds(start, c), :]
        pstart = pl.multiple_of(jnp.maximum(start - 8, 0), 8)
        nstart = pl.multiple_of(jnp.minimum(start + c, seq - 8), 8)
        prow = jnp.where(is_first, 0.0, ref[sq, pl.ds(pstart, 8), :][7:8])
        nrow = jnp.where(is_last, 0.0, ref[sq, pl.ds(nstart, 8), :][0:1])
        xp = jnp.where(rowc == 0, prow, pltpu.roll(x, 1, 0))
        xn = jnp.where(rowc == c - 1, nrow, pltpu.roll(x, c - 1, 0))
        return x + mu[0, j] * (xp - x) + mu[1, j] * (xn - x)

    tri_bf = [jnp.where(ccol <= crow, 1.0, 0.0).astype(BF16), jnp.where(ccol >= crow, 1.0, 0.0).astype(BF16)]
    n_levels = c.bit_length() - 1

    def level_mask(lvl, d):
        s = 1 << lvl
        siblings = jnp.logical_and(_div_pow2(rr, 2 * s) == _div_pow2(qq, 2 * s), same_head)
        r_late = (_div_pow2(rr, s) & 1) == 1
        q_late = (_div_pow2(qq, s) & 1) == 1
        if d == 0:
            return jnp.logical_and(siblings, jnp.logical_and(r_late, jnp.logical_not(q_late)))
        return jnp.logical_and(siblings, jnp.logical_and(jnp.logical_not(r_late), q_late))

    level_m = [[level_mask(lvl, d) for lvl in range(n_levels)] for d in range(2)]
    strict_m = [jnp.logical_and(qq < rr, same_head), jnp.logical_and(qq > rr, same_head)]
    incl_m = [jnp.logical_and(qq <= rr, same_head), jnp.logical_and(qq >= rr, same_head)]

    def phase_a(i, carry):
        chains = []
        for u in range(unroll):
            if nseq == 1:
                sq, n = 0, i * unroll + u
            else:
                sq, n = i * (unroll // n_chunks) + u // n_chunks, u % n_chunks
            start = pl.multiple_of(n * c, c)
            rows = pl.ds(start, c)
            is_first = n == 0
            is_last = n == n_chunks - 1
            r = shifted(r_ref, sq, 0, start, is_first, is_last)
            k = shifted(k_ref, sq, 1, start, is_first, is_last)
            v = shifted(v_ref, sq, 2, start, is_first, is_last)
            rkv_s[sq, 0, rows, :] = r
            rkv_s[sq, 1, rows, :] = k
            rkv_s[sq, 2, rows, :] = v
            kk = k * kkw
            kk = kk * lax.rsqrt(head_sum(kk * kk) + 1e-12)
            vs_h = stack(v).astype(BF16)
            for d in range(2):
                chains.append(dict(sq=sq, n=n, rows=rows, d=d, r=r, k=k, kk=kk, vs_h=vs_h))

        for ch in chains:
            d, rows, sq = ch["d"], ch["rows"], ch["sq"]
            l1w = l1_ref[sq, rows, d * LANES:(d + 1) * LANES]
            l1a = l1_ref[sq, rows, (2 + d) * LANES:(3 + d) * LANES]
            ch["dlog"] = w0_ref[d] + _dot(jnp.tanh(l1w).astype(BF16), w2_ref[d])
            ch["alog"] = a0_ref[d] + _dot(l1a.astype(BF16), a2_ref[d])
        for ch in chains:
            dlog = ch["dlog"]
            sp = jnp.maximum(-dlog, 0.0) + jnp.log(1.0 + jnp.exp(-jnp.abs(dlog)))
            ch["lw"] = -jnp.exp(-sp - 0.5)
            a = _sigmoid(ch["alog"])
            ch["kd"] = ch["k"] * (1.0 + (a - 1.0) * ka)
            ch["bb"] = a * ch["kk"]
        for ch in chains:
            ch["lcum"] = _dot_exact_lhs(tri_bf[ch["d"]], ch["lw"])
        for ch in chains:
            d, lcum = ch["d"], ch["lcum"]
            ltot = lcum[c - 1:c] if d == 0 else lcum[0:1]
            e_nl = jnp.exp(-lcum)
            e_rem = jnp.exp(ltot - lcum)
            ch["e_tot"] = jnp.exp(ltot)
            ch["rs"] = stack(ch["r"] * jnp.exp(lcum))
            ch["ks_b"] = stack(ch["kk"] * jnp.exp(lcum - ch["lw"])).astype(BF16)
            kbs_b = stack(ch["kd"] * e_nl).astype(BF16)
            bbs_b = stack(ch["bb"] * e_nl).astype(BF16)
            ch["kts_b"] = stack(ch["kd"] * e_rem).astype(BF16)
            ch["bts_b"] = stack(ch["bb"] * e_rem).astype(BF16)
            ch["lhs"] = jnp.concatenate([ch["ks_b"], ch["rs"].astype(BF16)], axis=0)
            ch["rhs"] = jnp.concatenate([kbs_b, bbs_b], axis=0)
        for ch in chains:
            ch["aa"] = _dot(ch["lhs"], ch["rhs"], _NT)
        for ch in chains:
            d, aa = ch["d"], ch["aa"]
            ch["a_kk"] = jnp.where(strict_m[d], aa[:c2, :c2], 0.0).astype(BF16)
            ch["a_rk"] = jnp.where(incl_m[d], aa[c2:, :c2], 0.0).astype(BF16)
            ch["a_rb"] = jnp.where(incl_m[d], aa[c2:, c2:], 0.0).astype(BF16)
            ch["a_kb"] = aa[:c2, c2:]
            ch["tinv"] = eye2 - jnp.where(level_m[d][0], ch["a_kb"], 0.0)
        for ch in chains:
            ch["av"] = _dot(ch["a_kk"], ch["vs_h"]).astype(BF16)
        for ch in chains:
            ch["vk"] = _dot(ch["vs_h"], ch["kts_b"], _TN)
        for lvl in range(1, n_levels):
            for ch in chains:
                ch["tb"] = ch["tinv"].astype(BF16)
                coupling = jnp.where(level_m[ch["d"]][lvl], ch["a_kb"], 0.0).astype(BF16)
                ch["e"] = _dot(coupling, ch["tb"]).astype(BF16)
            for ch in chains:
                ch["tinv"] = ch["tinv"] - _dot(ch["tb"], ch["e"])
        for ch in chains:
            wu = _dot(ch["tinv"].astype(BF16), jnp.concatenate([ch["ks_b"], ch["av"]], axis=1))
            ch["wu_b"] = wu.astype(BF16)
        for ch in chains:
            ch["ru"] = _dot(ch["a_rb"], ch["wu_b"])
            ch["y0"] = _dot(ch["a_rk"], ch["vs_h"])
        for ch in chains:
            ch["wb"] = _dot(ch["wu_b"], ch["bts_b"], _TN)
        y0_sum = {}
        for ch in chains:
            d, n, rows, sq = ch["d"], ch["n"], ch["rows"], ch["sq"]
            rt = ch["rs"] - ch["ru"][:, :LANES]
            y0 = ch["y0"] - ch["ru"][:, LANES:]
            gmat = jnp.where(diag_l, ch["e_tot"], 0.0) - jnp.where(block_l, ch["wb"][:LANES], 0.0)
            dmat = jnp.where(block_l, ch["vk"] - ch["wb"][LANES:], 0.0)
            rt_s[sq, d, rows, :] = unstack(rt).astype(BF16)
            g_s[sq, d, n] = gmat.astype(BF16)
            ds_s[sq, d, n] = dmat
            if d == 0:
                y0_sum = unstack(y0)
            else:
                ys_s[sq, rows, :] = y0_sum + unstack(y0)
        return carry

    total_chunks = nseq * n_chunks
    unroll = 8 if total_chunks % 8 == 0 else (4 if total_chunks % 4 == 0 else (2 if total_chunks % 2 == 0 else 1))
    assert nseq == 1 or unroll % n_chunks == 0
    lax.fori_loop(0, total_chunks // unroll, phase_a, 0)

    for sq in range(nseq):
        for d in range(2):
            st_s[sq, d] = s0_ref[sq, d, 0]
    lng = lng_ref[...]
    lnb = lnb_ref[...]
    rkw = rk_ref[...]
    g2 = g2_ref[...]

    def finish(sq, rows):
        y = ys_s[sq, rows, :]
        mean = head_sum(y) * (1.0 / hn)
        yc = y - mean
        var = head_sum(yc * yc) * (1.0 / hn)
        yn = yc * lax.rsqrt(var + RWKV_LN_EPS) * lng + lnb
        r = rkv_s[sq, 0, rows, :]
        k = rkv_s[sq, 1, rows, :]
        v = rkv_s[sq, 2, rows, :]
        bonus = head_sum(r * k * rkw) * v
        gate = _dot(_sigmoid(l1g_ref[sq, rows, :]).astype(BF16), g2)
        o_ref[sq, rows, :] = ((yn + bonus) * gate).astype(o_ref.dtype)

    def chunk_rows(idx):
        return pl.ds(pl.multiple_of(idx * c, c), c)

    def state_step(i):
        for sq in range(nseq):
            for d in range(2):
                idx = i if d == 0 else n_chunks - 1 - i
                rows = chunk_rows(idx)
                st_b = st_s[sq, d].astype(BF16)
                ys_s[sq, rows, :] = ys_s[sq, rows, :] + _dot(rt_s[sq, d, rows, :], st_b, _NT)
                st_s[sq, d] = _dot(st_b, g_s[sq, d, idx]) + ds_s[sq, d, idx]

    def phase_b_first(i, carry):
        state_step(i)
        return carry

    def phase_b_second(i, carry):
        for sq in range(nseq):
            finish(sq, chunk_rows(i - 1))
            finish(sq, chunk_rows(n_chunks - i))
        state_step(i)
        return carry

    assert n_chunks % 2 == 0
    half = n_chunks // 2
    lax.fori_loop(0, half + 1, phase_b_first, 0)
    lax.fori_loop(half + 1, n_chunks, phase_b_second, 0)
    for sq in range(nseq):
        finish(sq, chunk_rows(n_chunks - 1))
        finish(sq, chunk_rows(0))
    if want_sf:
        for sq in range(nseq):
            for d in range(2):
                sf_ref[sq, d, 0] = st_s[sq, d]


def _rwkv(proj, w2p, a2p, g2, w0, a0, mu, kkw, ka, rkw, lng, lnb, s0_bd, want_sf, col_rkv, col_l1, col_g):
    b, t, _ = proj.shape
    width = kkw.shape[-1]
    pairs = width // LANES
    c = RWKV_CHUNK
    n_chunks = t // c
    nseq = 1
    for target in (16, 8):
        if n_chunks < target and target % n_chunks == 0 and b % (target // n_chunks) == 0:
            nseq = target // n_chunks
            break
    cb = col_rkv // LANES
    vec = lambda: pl.BlockSpec((1, LANES), lambda bi, h: (0, h))
    in_specs = [
        pl.BlockSpec((nseq, t, LANES), lambda bi, h: (bi, 0, cb + h)),
        pl.BlockSpec((nseq, t, LANES), lambda bi, h: (bi, 0, cb + pairs + h)),
        pl.BlockSpec((nseq, t, LANES), lambda bi, h: (bi, 0, cb + 2 * pairs + h)),
        pl.BlockSpec((nseq, t, 4 * LANES), lambda bi, h: (bi, 0, col_l1 // (4 * LANES))),
        pl.BlockSpec((nseq, t, 2 * LANES), lambda bi, h: (bi, 0, col_g // (2 * LANES))),
        pl.BlockSpec((2, LANES, LANES), lambda bi, h: (0, 0, h)),
        pl.BlockSpec((2, LANES, LANES), lambda bi, h: (0, 0, h)),
        pl.BlockSpec((2 * LANES, LANES), lambda bi, h: (0, h)),
        pl.BlockSpec((2, 1, LANES), lambda bi, h: (0, 0, h)),
        pl.BlockSpec((2, 1, LANES), lambda bi, h: (0, 0, h)),
        pl.BlockSpec((2, 3, 1, LANES), lambda bi, h: (0, 0, 0, h)),
        vec(), vec(), vec(), vec(), vec(),
        pl.BlockSpec((nseq, 2, 1, LANES, LANES), lambda bi, h: (bi, 0, h, 0, 0)),
    ]
    out_specs = [pl.BlockSpec((nseq, t, LANES), lambda bi, h: (bi, 0, h))]
    out_shape = [jax.ShapeDtypeStruct((b, t, width), BF16)]
    if want_sf:
        out_specs.append(pl.BlockSpec((nseq, 2, 1, LANES, LANES), lambda bi, h: (bi, 0, h, 0, 0)))
        out_shape.append(jax.ShapeDtypeStruct((b, 2, pairs, LANES, LANES), F32))
    res = pl.pallas_call(
        functools.partial(_rwkv_kernel, seq=t, chunk=c, nseq=nseq, want_sf=want_sf),
        grid=(b // nseq, pairs),
        in_specs=in_specs,
        out_specs=out_specs,
        out_shape=out_shape,
        scratch_shapes=[pltpu.VMEM((nseq, 3, t, LANES), F32),
                        pltpu.VMEM((nseq, 2, t, LANES), BF16),
                        pltpu.VMEM((nseq, t, LANES), F32),
                        pltpu.VMEM((nseq, 2, n_chunks, LANES, LANES), BF16),
                        pltpu.VMEM((nseq, 2, n_chunks, LANES, LANES), F32),
                        pltpu.VMEM((nseq, 2, LANES, LANES), F32)],
        compiler_params=_cparams(2),
        name="rwkv7_chunked",
    )(proj, proj, proj, proj, proj, w2p, a2p, g2, w0, a0, mu, kkw, ka, rkw, lng, lnb, s0_bd)
    return res if want_sf else (res[0], None)


def _rope(x, cos, sin, first_half):
    part = jnp.where(first_half, pltpu.roll(x, x.shape[-1] - 32, 1), pltpu.roll(x, 32, 1))
    return x * cos + part * sin


def _diff_attn_kernel(*refs, rope, has_cache, lam_init, tq):
    q_ref, k_ref, v_ref, lq_ref, lk_ref, sub_ref = refs[:6]
    pos = 6
    if rope:
        cos_ref, sin_ref = refs[pos:pos + 2]
        pos += 2
    if has_cache:
        ck_ref, cv_ref = refs[pos:pos + 2]
        pos += 2
    o_ref = refs[pos]
    ks_ref, vs_ref = refs[pos + 1:pos + 3]

    t = k_ref.shape[2]
    w = k_ref.shape[3]
    dh = DIFF_DH
    i = pl.program_id(2)
    lane = lax.broadcasted_iota(jnp.int32, (1, w), 1)
    first_half = (lane & 63) < 32

    @pl.when(i == 0)
    def _():
        kb = min(t, 512)

        def fill(j, carry):
            rows = pl.ds(pl.multiple_of(j * kb, kb), kb)
            kx = k_ref[0, 0, rows, :]
            if rope:
                kx = _rope(kx, cos_ref[rows, :], sin_ref[rows, :], first_half)
            ks_ref[rows, :] = kx.astype(BF16)
            vs_ref[rows, :] = v_ref[0, 0, rows, :].astype(BF16)
            return carry

        lax.fori_loop(0, t // kb, fill, 0)
        if has_cache:
            ks_ref[t:, :] = ck_ref[0, 0, 0].astype(BF16)
            vs_ref[t:, :] = cv_ref[0, 0, 0].astype(BF16)

    q = q_ref[0, 0]
    if rope:
        qrows = pl.ds(pl.multiple_of(i * tq, tq), tq)
        q = _rope(q, cos_ref[qrows, :], sin_ref[qrows, :], first_half)
    qb = q.astype(BF16)
    lq = lq_ref[...]
    lk = lk_ref[...]
    lam = (jnp.exp(jnp.sum(lq[0:1] * lk[0:1], axis=-1, keepdims=True))
           - jnp.exp(jnp.sum(lq[1:2] * lk[1:2], axis=-1, keepdims=True)) + lam_init)
    kall = ks_ref[...]
    vall = vs_ref[...]
    exp2_scale = dh ** -0.5 * math.log2(math.e)
    sub = min(tq, 128)
    scores = [[_dot(qb[r:r + sub, m * dh:(m + 1) * dh], kall[:, m * dh:(m + 1) * dh], _NT) for m in range(2)]
              for r in range(0, tq, sub)]
    rows_out = []
    for pair in scores:
        es, invs = [], []
        for s in pair:
            e = jnp.exp2((s - jnp.max(s, axis=-1, keepdims=True)) * exp2_scale)
            es.append(e)
            invs.append(1.0 / jnp.sum(e, axis=-1, keepdims=True))
        p = es[0] * invs[0] - es[1] * (lam * invs[1])
        rows_out.append(_dot(p.astype(BF16), vall))
    o = jnp.concatenate(rows_out, axis=0) if len(rows_out) > 1 else rows_out[0]
    on = o * lax.rsqrt(jnp.mean(o * o, axis=-1, keepdims=True) + SUBLN_EPS) * sub_ref[...]
    o_ref[0] = (on * (1.0 - lam_init)).astype(o_ref.dtype)


def _diff_attention(q, k, v, lq, lk, subln, lam_init, rope_tabs, cache_k, cache_v, tq=512):
    b, heads, t, w = q.shape
    tq = _pick(t, tq)
    rope = rope_tabs is not None
    has_cache = cache_k is not None
    tk = t + (cache_k.shape[3] if has_cache else 0)
    in_specs = [
        pl.BlockSpec((1, 1, tq, w), lambda bi, h, i: (bi, h, i, 0)),
        pl.BlockSpec((1, 1, t, w), lambda bi, h, i: (bi, h, 0, 0)),
        pl.BlockSpec((1, 1, t, w), lambda bi, h, i: (bi, h, 0, 0)),
        pl.BlockSpec((2, DIFF_DH), lambda bi, h, i: (0, 0)),
        pl.BlockSpec((2, DIFF_DH), lambda bi, h, i: (0, 0)),
        pl.BlockSpec((1, w), lambda bi, h, i: (0, 0)),
    ]
    args = [q, k, v, lq, lk, subln.reshape(1, w)]
    if rope:
        in_specs += [pl.BlockSpec((t, w), lambda bi, h, i: (0, 0))] * 2
        args += list(rope_tabs)
    if has_cache:
        pc = cache_k.shape[3]
        in_specs += [pl.BlockSpec((1, 1, 1, pc, w), lambda bi, h, i: (bi, 0, h, 0, 0))] * 2
        args += [cache_k, cache_v]
    return pl.pallas_call(
        functools.partial(_diff_attn_kernel, rope=rope, has_cache=has_cache, lam_init=lam_init, tq=tq),
        grid=(b, heads, t // tq),
        in_specs=in_specs,
        out_specs=pl.BlockSpec((1, tq, w), lambda bi, h, i: (bi, i, h)),
        out_shape=jax.ShapeDtypeStruct((b, t, heads * w), BF16),
        scratch_shapes=[pltpu.VMEM((tk, w), BF16), pltpu.VMEM((tk, w), BF16)],
        compiler_params=_cparams(3),
        name="diff_attention",
    )(*args)


def _rope_tables(t, w):
    nf = DIFF_DH // 4
    rows = t // GRID_W
    row = jnp.repeat(jnp.arange(rows, dtype=F32), GRID_W)
    col = jnp.tile(jnp.arange(GRID_W, dtype=F32), rows)
    inv = jnp.power(ROPE_BASE, -jnp.arange(nf, dtype=F32) / nf)
    ang_r = row[:, None] * inv
    ang_c = col[:, None] * inv
    cos = jnp.concatenate([jnp.cos(ang_r), jnp.cos(ang_r), jnp.cos(ang_c), jnp.cos(ang_c)], axis=-1)
    sin = jnp.concatenate([-jnp.sin(ang_r), jnp.sin(ang_r), -jnp.sin(ang_c), jnp.sin(ang_c)], axis=-1)
    reps = w // DIFF_DH
    return jnp.tile(cos, (1, reps)), jnp.tile(sin, (1, reps))


def _pad_rows(w, rows):
    return jnp.pad(w, ((0, 0),) * (w.ndim - 2) + ((0, rows - w.shape[-2]), (0, 0)))


def _pad_cols(w, cols):
    return jnp.pad(w, ((0, 0),) * (w.ndim - 1) + ((0, cols - w.shape[-1]),))


def kernel(x_prompt, x_sample, state_gla, state_rwkv, cache_k, cache_v, c, c_ctx, ada_w, ada_b, norm_mix, norm_ffn, norm_final, ffn_w1, ffn_w3, ffn_w2, even_w_in, even_w_out, gla_a1, gla_a2, gla_ab, gla_norm, rw_mu, rw_w0, rw_w1, rw_w2, rw_a0, rw_a1, rw_a2, rw_g1, rw_g2, rw_kk, rw_ka, rw_rk, rw_ln_g, rw_ln_b, odd_w_in, odd_w_out, diff_lq, diff_lk, diff_subln):
    depth, d, _ = ada_w.shape
    dec_b = x_sample.shape[0]
    gla_heads, gla_dk, gla_dv = state_gla.shape[3:]
    rw_heads = state_rwkv.shape[3]
    rw_w = rw_heads * RWKV_N
    pairs = rw_w // LANES
    diff_heads, _, diff_w = cache_k.shape[2:]
    gla_qk = gla_heads * gla_dk
    gla_vw = gla_heads * gla_dv
    even_in = even_w_in.shape[-1]

    nb = -(-(1 + dec_b) // 8) * 8
    cond = jnp.zeros((nb, d), F32).at[0].set(c_ctx).at[1:1 + dec_b].set(c)
    mod = _adaln(cond, ada_w, ada_b)

    def mods(layer, prompt):
        m = mod[layer, 0:1] if prompt else mod[layer, 1:1 + dec_b]
        return [m[:, None, k * d:(k + 1) * d] for k in range(6)]

    def even_weights(e):
        lora1 = jnp.concatenate(
            [_pad_cols(rw_w1[e, 0], LANES), _pad_cols(rw_w1[e, 1], LANES),
             _pad_cols(rw_a1[e, 0], LANES), _pad_cols(rw_a1[e, 1], LANES),
             rw_g1[e],
             _pad_cols(gla_a1[e, 0], LANES), _pad_cols(gla_a1[e, 1], LANES)], axis=-1)
        w_cat = jnp.concatenate([even_w_in[e], lora1], axis=-1).astype(BF16)
        return dict(
            w_cat=w_cat,
            w_out=even_w_out[e].astype(BF16),
            gla_a2=_pad_rows(gla_a2[e], LANES),
            gla_ab=gla_ab[e].reshape(2, 1, gla_qk),
            gla_norm=gla_norm[e].reshape(1, gla_dv),
            w2=_pad_rows(rw_w2[e], LANES).astype(BF16),
            a2=_pad_rows(rw_a2[e], LANES).astype(BF16),
            g2=rw_g2[e].astype(BF16),
            w0=rw_w0[e].reshape(2, 1, rw_w),
            a0=rw_a0[e].reshape(2, 1, rw_w),
            mu=rw_mu[e].reshape(2, 3, 1, rw_w),
            kk=rw_kk[e].reshape(1, rw_w), ka=rw_ka[e].reshape(1, rw_w), rk=rw_rk[e].reshape(1, rw_w),
            lng=rw_ln_g[e].reshape(1, rw_w), lnb=rw_ln_b[e].reshape(1, rw_w),
        )

    col_rkv = 2 * gla_qk + 2 * gla_vw
    col_l1 = even_in
    col_g = even_in + 4 * LANES
    col_gl = even_in + 6 * LANES

    def block_diag_state(s):
        bsz = s.shape[0]
        s = s.reshape(bsz, 2, pairs, 2, RWKV_N, RWKV_N)
        z = jnp.zeros_like(s[:, :, :, 0])
        top = jnp.concatenate([s[:, :, :, 0], z], axis=-1)
        bot = jnp.concatenate([z, s[:, :, :, 1]], axis=-1)
        return jnp.concatenate([top, bot], axis=-2)

    def diag_blocks(sbd):
        bsz = sbd.shape[0]
        h0 = sbd[:, :, :, :RWKV_N, :RWKV_N]
        h1 = sbd[:, :, :, RWKV_N:, RWKV_N:]
        return jnp.stack([h0, h1], axis=3).reshape(bsz, 2, rw_heads, RWKV_N, RWKV_N)

    def even_mixer(h, ew, s_gla, s_rwkv, want_states):
        proj = _matmul(h, ew["w_cat"], tn=1792, name="even_in_proj")
        o_gla, sg = _gla(proj, ew["gla_a2"], ew["gla_ab"], ew["gla_norm"], s_gla, want_states,
                         gla_heads, gla_dk, gla_dv, col_gl)
        if s_rwkv is None:
            s_bd = jnp.zeros((h.shape[0], 2, pairs, LANES, LANES), F32)
        else:
            s_bd = block_diag_state(s_rwkv)
        y_rw, sr = _rwkv(proj, ew["w2"], ew["a2"], ew["g2"], ew["w0"], ew["a0"], ew["mu"], ew["kk"], ew["ka"],
                         ew["rk"], ew["lng"], ew["lnb"], s_bd, want_states, col_rkv, col_l1, col_g)
        mix = [o_gla, y_rw]
        return mix, ew["w_out"], sg, (diag_blocks(sr) if want_states else None)

    def run_group(x, prompt):
        outs = {}
        for layer in range(depth):
            sh1, sc1, g1, sh2, sc2, g2 = mods(layer, prompt)
            h = _norm_mod(x, norm_mix[layer], sc1, sh1)
            if layer % 2 == 0:
                e = layer // 2
                ew = even_weights(e)
                if prompt:
                    mix, w_out, sg, sr = even_mixer(h, ew, None, None, True)
                    outs.setdefault("gla", []).append(sg)
                    outs.setdefault("rwkv", []).append(sr)
                else:
                    mix, w_out, _, _ = even_mixer(h, ew, state_gla[:, e], state_rwkv[:, e], False)
            else:
                o = layer // 2
                lam_init = 0.8 - 0.6 * math.exp(-0.3 * layer)
                q, k, v = _matmul_heads(h, odd_w_in[o].astype(BF16), diff_heads, diff_w)
                if prompt:
                    outs.setdefault("k", []).append(k)
                    outs.setdefault("v", []).append(v)
                    mix = _diff_attention(q, k, v, diff_lq[o], diff_lk[o], diff_subln[o], lam_init,
                                          None, None, None)
                else:
                    mix = _diff_attention(q, k, v, diff_lq[o], diff_lk[o], diff_subln[o], lam_init,
                                          _rope_tables(x.shape[1], diff_w), cache_k[:, o:o + 1], cache_v[:, o:o + 1])
                w_out = odd_w_out[o].astype(BF16)
            x = _matmul_residual(mix, w_out, x, g1, tm=512, tn=2048, name="mix_out_proj")
            h = _norm_mod(x, norm_ffn[layer], sc2, sh2)
            u = _matmul_swiglu(h, ffn_w1[layer].astype(BF16), ffn_w3[layer].astype(BF16))
            x = _matmul_residual(u, ffn_w2[layer].astype(BF16), x, g2, name="ffn_down_proj")
        return _final_norm(x, norm_final), outs

    y_prompt, po = run_group(x_prompt, True)
    y_sample, _ = run_group(x_sample, False)
    return (y_prompt, y_sample,
            jnp.stack(po["gla"], axis=1), jnp.stack(po["rwkv"], axis=1),
            jnp.stack(po["k"], axis=1), jnp.stack(po["v"], axis=1))
```

```python
import functools
import math

import jax
import jax.numpy as jnp
from jax import lax
from jax.experimental import pallas as pl
from jax.experimental.pallas import tpu as pltpu

F32 = jnp.float32
BF16 = jnp.bfloat16

V7X_VMEM_LIMIT_BYTES = 56 * 1024 * 1024
LANES = 128

NORM_EPS = 1e-6
GLA_GATE_NORM = 16.0
GLA_CHUNK = 64
RWKV_CHUNK = 64
RWKV_N = 64
RWKV_LN_EPS = 64e-5
DIFF_DH = 128
GRID_W = 64
ROPE_BASE = 10000.0
SUBLN_EPS = 1e-5


def _cparams(n_axes):
    return pltpu.CompilerParams(dimension_semantics=("arbitrary",) * n_axes,
                                vmem_limit_bytes=V7X_VMEM_LIMIT_BYTES)


def _dot(a, b, dims=(((1,), (0,)), ((), ()))):
    return lax.dot_general(a, b, dims, preferred_element_type=F32)


_NN = (((1,), (0,)), ((), ()))
_NT = (((1,), (1,)), ((), ()))
_TN = (((0,), (0,)), ((), ()))


def _split3(x):
    h = x.astype(BF16)
    r = x - h.astype(F32)
    m = r.astype(BF16)
    l = (r - m.astype(F32)).astype(BF16)
    return h, m, l


def _dot_exact_lhs(a_bf, b, dims=_NN):
    h, m, l = _split3(b)
    return _dot(a_bf, h, dims) + (_dot(a_bf, m, dims) + _dot(a_bf, l, dims))


def _div_pow2(x, n):
    assert n & (n - 1) == 0
    return lax.shift_right_logical(x, n.bit_length() - 1)


def _sigmoid(x):
    return 1.0 / (1.0 + jnp.exp(-x))


def _log_sigmoid(z):
    return jnp.minimum(z, 0.0) - jnp.log(1.0 + jnp.exp(-jnp.abs(z)))


def _adaln_kernel(c_ref, w_ref, b_ref, o_ref):
    cs = c_ref[...]
    s = cs * _sigmoid(cs)
    o_ref[0] = _dot(s.astype(BF16), w_ref[0].astype(BF16)) + b_ref[0]


def _adaln(cond, ada_w, ada_b, tn=1024):
    depth, d, n = ada_w.shape
    nb = cond.shape[0]
    return pl.pallas_call(
        _adaln_kernel,
        grid=(depth, n // tn),
        in_specs=[pl.BlockSpec((nb, d), lambda l, j: (0, 0)),
                  pl.BlockSpec((1, d, tn), lambda l, j: (l, 0, j)),
                  pl.BlockSpec((1, 1, tn), lambda l, j: (l, 0, j))],
        out_specs=pl.BlockSpec((1, nb, tn), lambda l, j: (l, 0, j)),
        out_shape=jax.ShapeDtypeStruct((depth, nb, n), F32),
        compiler_params=_cparams(2),
        name="adaln_mod",
    )(cond, ada_w, ada_b.reshape(depth, 1, n))


def _norm_mod_kernel(x_ref, g_ref, sc_ref, sh_ref, o_ref):
    x = x_ref[0]
    xn = x * lax.rsqrt(jnp.mean(x * x, axis=-1, keepdims=True) + NORM_EPS)
    o_ref[0] = ((xn * g_ref[...]) * (1.0 + sc_ref[0]) + sh_ref[0]).astype(o_ref.dtype)


def _norm_mod(x, g, sc, sh, tt=1024):
    b, t, d = x.shape
    tt = _pick(t, tt)
    per_batch = sc.shape[0] > 1
    mod_map = (lambda i, j: (i, 0, 0)) if per_batch else (lambda i, j: (0, 0, 0))
    return pl.pallas_call(
        _norm_mod_kernel,
        grid=(b, t // tt),
        in_specs=[pl.BlockSpec((1, tt, d), lambda i, j: (i, j, 0)),
                  pl.BlockSpec((1, d), lambda i, j: (0, 0)),
                  pl.BlockSpec((1, 1, d), mod_map),
                  pl.BlockSpec((1, 1, d), mod_map)],
        out_specs=pl.BlockSpec((1, tt, d), lambda i, j: (i, j, 0)),
        out_shape=jax.ShapeDtypeStruct((b, t, d), BF16),
        compiler_params=_cparams(2),
        name="norm_mod",
    )(x, g.reshape(1, d), sc, sh)


def _final_norm_kernel(x_ref, g_ref, o_ref):
    x = x_ref[0]
    xn = x * lax.rsqrt(jnp.mean(x * x, axis=-1, keepdims=True) + NORM_EPS)
    o_ref[0] = xn * g_ref[...]


def _final_norm(x, g, tt=1024):
    b, t, d = x.shape
    tt = _pick(t, tt)
    return pl.pallas_call(
        _final_norm_kernel,
        grid=(b, t // tt),
        in_specs=[pl.BlockSpec((1, tt, d), lambda i, j: (i, j, 0)),
                  pl.BlockSpec((1, d), lambda i, j: (0, 0))],
        out_specs=pl.BlockSpec((1, tt, d), lambda i, j: (i, j, 0)),
        out_shape=jax.ShapeDtypeStruct((b, t, d), F32),
        compiler_params=_cparams(2),
        name="final_norm",
    )(x, g.reshape(1, d))


def _mm_kernel(x_ref, w_ref, o_ref):
    o_ref[...] = _dot(x_ref[...], w_ref[...]).astype(o_ref.dtype)


def _pick(n, pref):
    t = min(n, pref)
    while n % t:
        t //= 2
    return t


def _matmul(x, w, out_dtype=F32, tm=1024, tn=512, name="matmul"):
    b, t, k = x.shape
    n = w.shape[1]
    m = b * t
    tm, tn = _pick(m, tm), _pick(n, tn)
    out = pl.pallas_call(
        _mm_kernel,
        grid=(m // tm, n // tn),
        in_specs=[pl.BlockSpec((tm, k), lambda i, j: (i, 0)),
                  pl.BlockSpec((k, tn), lambda i, j: (0, j))],
        out_specs=pl.BlockSpec((tm, tn), lambda i, j: (i, j)),
        out_shape=jax.ShapeDtypeStruct((m, n), out_dtype),
        compiler_params=_cparams(2),
        name=name,
    )(x.reshape(m, k), w)
    return out.reshape(b, t, n)


def _mm_res_kernel(*refs, n_in):
    r_ref, g_ref, o_ref = refs[2 * n_in:]
    acc = _dot(refs[0][...], refs[n_in][...])
    for i in range(1, n_in):
        acc = acc + _dot(refs[i][...], refs[n_in + i][...])
    o_ref[...] = r_ref[...] + g_ref[0] * acc


def _matmul_residual(xs, w, res, gate, tm=1024, tn=512, name="matmul_residual"):
    xs = xs if isinstance(xs, (list, tuple)) else [xs]
    b, t, k = xs[0].shape
    n_in = len(xs)
    n = w.shape[1]
    m = b * t
    per_batch = gate.shape[0] > 1
    tm = _pick(t if per_batch else m, tm)
    tn = _pick(n, tn)
    gate_map = (lambda i, j: ((i * tm) // t, 0, j)) if per_batch else (lambda i, j: (0, 0, j))
    out = pl.pallas_call(
        functools.partial(_mm_res_kernel, n_in=n_in),
        grid=(m // tm, n // tn),
        in_specs=([pl.BlockSpec((tm, k), lambda i, j: (i, 0))] * n_in
                  + [pl.BlockSpec((k, tn), functools.partial(lambda i, j, p: (p, j), p=p)) for p in range(n_in)]
                  + [pl.BlockSpec((tm, tn), lambda i, j: (i, j)),
                     pl.BlockSpec((1, 1, tn), gate_map)]),
        out_specs=pl.BlockSpec((tm, tn), lambda i, j: (i, j)),
        out_shape=jax.ShapeDtypeStruct((m, n), F32),
        compiler_params=_cparams(2),
        name=name,
    )(*[x.reshape(m, k) for x in xs], *([w] * n_in), res.reshape(m, n), gate)
    return out.reshape(b, t, n)


def _mm_swiglu_kernel(x_ref, w1_ref, w3_ref, o_ref):
    x = x_ref[...]
    a = _dot(x, w1_ref[...])
    o_ref[...] = ((a * _sigmoid(a)) * _dot(x, w3_ref[...])).astype(o_ref.dtype)


def _matmul_swiglu(x, w1, w3, tm=1024, tn=512):
    b, t, k = x.shape
    n = w1.shape[1]
    m = b * t
    tm, tn = _pick(m, tm), _pick(n, tn)
    out = pl.pallas_call(
        _mm_swiglu_kernel,
        grid=(m // tm, n // tn),
        in_specs=[pl.BlockSpec((tm, k), lambda i, j: (i, 0)),
                  pl.BlockSpec((k, tn), lambda i, j: (0, j)),
                  pl.BlockSpec((k, tn), lambda i, j: (0, j))],
        out_specs=pl.BlockSpec((tm, tn), lambda i, j: (i, j)),
        out_shape=jax.ShapeDtypeStruct((m, n), BF16),
        compiler_params=_cparams(2),
        name="swiglu_up",
    )(x.reshape(m, k), w1, w3)
    return out.reshape(b, t, n)


def _mm_heads_kernel(x_ref, wq_ref, wk_ref, wv_ref, q_ref, k_ref, v_ref, *, nbt, rows, hb, hw):
    x = x_ref[...]
    for w_ref, o_ref in ((wq_ref, q_ref), (wk_ref, k_ref), (wv_ref, v_ref)):
        r = _dot(x, w_ref[...])
        for bb in range(nbt):
            for i in range(hb):
                o_ref[bb, i] = r[bb * rows:(bb + 1) * rows, i * hw:(i + 1) * hw]


def _matmul_heads(x, w, heads, hw, hb=2, tm=1024):
    b, t, k = x.shape
    m = b * t
    tm = _pick(m, tm)
    if tm >= t:
        nbt, rows = tm // t, t
        out_map = lambda i, j: (i, j, 0, 0)
    else:
        nbt, rows = 1, tm
        per_seq = t // tm
        out_map = lambda i, j: (i // per_seq, j, i % per_seq, 0)
    hblocks = heads // hb
    out_spec = pl.BlockSpec((nbt, hb, rows, hw), out_map)
    return pl.pallas_call(
        functools.partial(_mm_heads_kernel, nbt=nbt, rows=rows, hb=hb, hw=hw),
        grid=(m // tm, hblocks),
        in_specs=[pl.BlockSpec((tm, k), lambda i, j: (i, 0)),
                  pl.BlockSpec((k, hb * hw), lambda i, j: (0, j)),
                  pl.BlockSpec((k, hb * hw), lambda i, j: (0, hblocks + j)),
                  pl.BlockSpec((k, hb * hw), lambda i, j: (0, 2 * hblocks + j))],
        out_specs=[out_spec] * 3,
        out_shape=[jax.ShapeDtypeStruct((b, heads, t, hw), F32)] * 3,
        compiler_params=_cparams(2),
        name="qkv_heads",
    )(x.reshape(m, k), w, w, w)


def _gla_kernel(*refs, seq, chunk, dk, has_s0, want_sf):
    q_ref, k_ref, v_ref, gg_ref, l1_ref, a2_ref, ab_ref, gn_ref = refs[:8]
    pos = 8
    s0_ref = None
    if has_s0:
        s0_ref = refs[pos]
        pos += 1
    o_ref = refs[pos]
    pos += 1
    sf_ref = None
    if want_sf:
        sf_ref = refs[pos]
        pos += 1
    acc_ref = refs[pos]

    n_chunks = seq // chunk
    dv = v_ref.shape[-1]
    scale = dk ** -0.5
    row = lax.broadcasted_iota(jnp.int32, (chunk, chunk), 0)
    col = lax.broadcasted_iota(jnp.int32, (chunk, chunk), 1)

    tri = [col <= row, col >= row]
    tri_bf = [jnp.where(t, 1.0, 0.0).astype(BF16) for t in tri]
    a2 = [a2_ref[d].astype(BF16) for d in range(2)]
    ab = [ab_ref[d] for d in range(2)]
    if has_s0:
        st0 = tuple(s0_ref[0, d, 0].T for d in range(2))
    else:
        st0 = tuple(jnp.zeros((dv, dk), F32) for d in range(2))
    unroll = 8 if n_chunks % 8 == 0 else (4 if n_chunks % 4 == 0 else (2 if n_chunks % 2 == 0 else 1))
    rb = min(seq, 256)

    def zero(i, carry):
        acc_ref[pl.ds(pl.multiple_of(i * rb, rb), rb), :] = jnp.zeros((rb, dv), F32)
        return carry

    lax.fori_loop(0, seq // rb, zero, 0)

    def body(i, carry):
        st = list(carry)
        chains = []
        for u in range(unroll):
            for d in range(2):
                n = i * unroll + u
                idx = n if d == 0 else n_chunks - 1 - n
                chains.append(dict(d=d, rows=pl.ds(pl.multiple_of(idx * chunk, chunk), chunk)))
        for ch in chains:
            d, rows = ch["d"], ch["rows"]
            l1 = l1_ref[0, rows, d * LANES:(d + 1) * LANES]
            ch["z"] = _dot(l1.astype(BF16), a2[d]) + ab[d]
        for ch in chains:
            ch["la"] = _log_sigmoid(ch["z"]) / GLA_GATE_NORM
        for ch in chains:
            ch["cum"] = _dot_exact_lhs(tri_bf[ch["d"]], ch["la"])
        for ch in chains:
            d, rows, cum = ch["d"], ch["rows"], ch["cum"]
            cl = cum[chunk - 1:chunk] if d == 0 else cum[0:1]
            k = k_ref[0, rows, :]
            ch["qe"] = ((q_ref[0, rows, :] * scale) * jnp.exp(cum)).astype(BF16)
            ch["ke"] = (k * jnp.exp(-cum)).astype(BF16)
            ch["kdec"] = (k * jnp.exp(cl - cum)).astype(BF16)
            ch["dec"] = jnp.exp(cl)
            ch["vb"] = v_ref[0, rows, :].astype(BF16)
        for ch in chains:
            ch["att"] = jnp.where(tri[ch["d"]], _dot(ch["qe"], ch["ke"], _NT), 0.0).astype(BF16)
        for ch in chains:
            ch["ds"] = _dot(ch["vb"], ch["kdec"], _TN)
            ch["o"] = _dot(ch["att"], ch["vb"])
        for ch in chains:
            d = ch["d"]
            ch["st_in"] = st[d].astype(BF16)
            st[d] = st[d] * ch["dec"] + ch["ds"]
        for ch in chains:
            rows = ch["rows"]
            acc_ref[rows, :] = acc_ref[rows, :] + (ch["o"] + _dot(ch["qe"], ch["st_in"], _NT))
        return tuple(st)

    st = lax.fori_loop(0, n_chunks // unroll, body, st0)
    if want_sf:
        for d in range(2):
            sf_ref[0, d, 0] = st[d].T

    gn = gn_ref[...]

    def epilogue(i, carry):
        rows = pl.ds(pl.multiple_of(i * rb, rb), rb)
        o = acc_ref[rows, :]
        on = o * lax.rsqrt(jnp.mean(o * o, axis=-1, keepdims=True) + NORM_EPS) * gn
        g = gg_ref[0, rows, :]
        o_ref[0, rows, :] = (on * (g * _sigmoid(g))).astype(o_ref.dtype)
        return carry

    lax.fori_loop(0, seq // rb, epilogue, 0)


def _gla(proj, a2p, ab, gnorm, s0, want_sf, heads, dk, dv, col_l1):
    b, t, _ = proj.shape
    has_s0 = s0 is not None
    in_specs = [
        pl.BlockSpec((1, t, dk), lambda bi, h: (bi, 0, h)),
        pl.BlockSpec((1, t, dk), lambda bi, h: (bi, 0, heads + h)),
        pl.BlockSpec((1, t, dv), lambda bi, h: (bi, 0, (2 * heads * dk) // dv + h)),
        pl.BlockSpec((1, t, dv), lambda bi, h: (bi, 0, (2 * heads * dk) // dv + heads + h)),
        pl.BlockSpec((1, t, 2 * LANES), lambda bi, h: (bi, 0, col_l1 // (2 * LANES))),
        pl.BlockSpec((2, LANES, dk), lambda bi, h: (0, 0, h)),
        pl.BlockSpec((2, 1, dk), lambda bi, h: (0, 0, h)),
        pl.BlockSpec((1, dv), lambda bi, h: (0, 0)),
    ]
    args = [proj, proj, proj, proj, proj, a2p, ab, gnorm]
    if has_s0:
        in_specs.append(pl.BlockSpec((1, 2, 1, dk, dv), lambda bi, h: (bi, 0, h, 0, 0)))
        args.append(s0)
    out_specs = [pl.BlockSpec((1, t, dv), lambda bi, h: (bi, 0, h))]
    out_shape = [jax.ShapeDtypeStruct((b, t, heads * dv), BF16)]
    if want_sf:
        out_specs.append(pl.BlockSpec((1, 2, 1, dk, dv), lambda bi, h: (bi, 0, h, 0, 0)))
        out_shape.append(jax.ShapeDtypeStruct((b, 2, heads, dk, dv), F32))
    res = pl.pallas_call(
        functools.partial(_gla_kernel, seq=t, chunk=GLA_CHUNK, dk=dk, has_s0=has_s0, want_sf=want_sf),
        grid=(b, heads),
        in_specs=in_specs,
        out_specs=out_specs,
        out_shape=out_shape,
        scratch_shapes=[pltpu.VMEM((t, dv), F32)],
        compiler_params=_cparams(2),
        name="gla_chunked",
    )(*args)
    return res if want_sf else (res[0], None)


def _rwkv_kernel(*refs, seq, chunk, nseq, want_sf):
    (r_ref, k_ref, v_ref, l1_ref, l1g_ref, w2_ref, a2_ref, g2_ref, w0_ref, a0_ref, mu_ref,
     kkw_ref, ka_ref, rk_ref, lng_ref, lnb_ref, s0_ref) = refs[:17]
    pos = 17
    o_ref = refs[pos]
    pos += 1
    sf_ref = None
    if want_sf:
        sf_ref = refs[pos]
        pos += 1
    rkv_s, rt_s, ys_s, g_s, ds_s, st_s = refs[pos:pos + 6]

    c = chunk
    c2 = 2 * chunk
    n_chunks = seq // c
    hn = RWKV_N

    lane = lax.broadcasted_iota(jnp.int32, (1, LANES), 1)
    head0 = lane < hn
    rowc = lax.broadcasted_iota(jnp.int32, (c, 1), 0)
    srow = lax.broadcasted_iota(jnp.int32, (c2, 1), 0)
    stack_mask = _div_pow2(srow, c) == _div_pow2(lane, hn)
    r2 = lax.broadcasted_iota(jnp.int32, (c2, c2), 0)
    q2 = lax.broadcasted_iota(jnp.int32, (c2, c2), 1)
    same_head = _div_pow2(r2, c) == _div_pow2(q2, c)
    rr = jnp.where(r2 >= c, r2 - c, r2)
    qq = jnp.where(q2 >= c, q2 - c, q2)
    eye2 = jnp.where(r2 == q2, 1.0, 0.0).astype(F32)
    lr = lax.broadcasted_iota(jnp.int32, (LANES, LANES), 0)
    lc = lax.broadcasted_iota(jnp.int32, (LANES, LANES), 1)
    diag_l = lr == lc
    block_l = _div_pow2(lr, hn) == _div_pow2(lc, hn)
    crow = lax.broadcasted_iota(jnp.int32, (c, c), 0)
    ccol = lax.broadcasted_iota(jnp.int32, (c, c), 1)

    def head_sum(x):
        s0 = jnp.sum(jnp.where(head0, x, 0.0), axis=-1, keepdims=True)
        s1 = jnp.sum(jnp.where(head0, 0.0, x), axis=-1, keepdims=True)
        return jnp.where(head0, s0, s1)

    def stack(x):
        return jnp.where(stack_mask, jnp.concatenate([x, x], axis=0), 0.0)

    def unstack(xs):
        return xs[:c] + xs[c:]

    mu = mu_ref[...]
    kkw = kkw_ref[...]
    ka = ka_ref[...]

    def shifted(ref, sq, j, start, is_first, is_last):
        x = ref[sq, pl.ds(start, c), :]
        pstart = pl.multiple_of(jnp.maximum(start - 8, 0), 8)
        nstart = pl.multiple_of(jnp.minimum(start + c, seq - 8), 8)
        prow = jnp.where(is_first, 0.0, ref[sq, pl.ds(pstart, 8), :][7:8])
        nrow = jnp.where(is_last, 0.0, ref[sq, pl.ds(nstart, 8), :][0:1])
        xp = jnp.where(rowc == 0, prow, pltpu.roll(x, 1, 0))
        xn = jnp.where(rowc == c - 1, nrow, pltpu.roll(x, c - 1, 0))
        return x + mu[0, j] * (xp - x) + mu[1, j] * (xn - x)

    tri_bf = [jnp.where(ccol <= crow, 1.0, 0.0).astype(BF16), jnp.where(ccol >= crow, 1.0, 0.0).astype(BF16)]
    n_levels = c.bit_length() - 1

    def level_mask(lvl, d):
        s = 1 << lvl
        siblings = jnp.logical_and(_div_pow2(rr, 2 * s) == _div_pow2(qq, 2 * s), same_head)
        r_late = (_div_pow2(rr, s) & 1) == 1
        q_late = (_div_pow2(qq, s) & 1) == 1
        if d == 0:
            return jnp.logical_and(siblings, jnp.logical_and(r_late, jnp.logical_not(q_late)))
        return jnp.logical_and(siblings, jnp.logical_and(jnp.logical_not(r_late), q_late))

    level_m = [[level_mask(lvl, d) for lvl in range(n_levels)] for d in range(2)]
    strict_m = [jnp.logical_and(qq < rr, same_head), jnp.logical_and(qq > rr, same_head)]
    incl_m = [jnp.logical_and(qq <= rr, same_head), jnp.logical_and(qq >= rr, same_head)]

    def phase_a(i, carry):
        chains = []
        for u in range(unroll):
            if nseq == 1:
                sq, n = 0, i * unroll + u
            else:
                sq, n = i * (unroll // n_chunks) + u // n_chunks, u % n_chunks
            start = pl.multiple_of(n * c, c)
            rows = pl.ds(start, c)
            is_first = n == 0
            is_last = n == n_chunks - 1
            r = shifted(r_ref, sq, 0, start, is_first, is_last)
            k = shifted(k_ref, sq, 1, start, is_first, is_last)
            v = shifted(v_ref, sq, 2, start, is_first, is_last)
            rkv_s[sq, 0, rows, :] = r
            rkv_s[sq, 1, rows, :] = k
            rkv_s[sq, 2, rows, :] = v
            kk = k * kkw
            kk = kk * lax.rsqrt(head_sum(kk * kk) + 1e-12)
            vs_h = stack(v).astype(BF16)
            for d in range(2):
                chains.append(dict(sq=sq, n=n, rows=rows, d=d, r=r, k=k, kk=kk, vs_h=vs_h))

        for ch in chains:
            d, rows, sq = ch["d"], ch["rows"], ch["sq"]
            l1w = l1_ref[sq, rows, d * LANES:(d + 1) * LANES]
            l1a = l1_ref[sq, rows, (2 + d) * LANES:(3 + d) * LANES]
            ch["dlog"] = w0_ref[d] + _dot(jnp.tanh(l1w).astype(BF16), w2_ref[d])
            ch["alog"] = a0_ref[d] + _dot(l1a.astype(BF16), a2_ref[d])
        for ch in chains:
            dlog = ch["dlog"]
            sp = jnp.maximum(-dlog, 0.0) + jnp.log(1.0 + jnp.exp(-jnp.abs(dlog)))
            ch["lw"] = -jnp.exp(-sp - 0.5)
            a = _sigmoid(ch["alog"])
            ch["kd"] = ch["k"] * (1.0 + (a - 1.0) * ka)
            ch["bb"] = a * ch["kk"]
        for ch in chains:
            ch["lcum"] = _dot_exact_lhs(tri_bf[ch["d"]], ch["lw"])
        for ch in chains:
            d, lcum = ch["d"], ch["lcum"]
            ltot = lcum[c - 1:c] if d == 0 else lcum[0:1]
            e_nl = jnp.exp(-lcum)
            e_rem = jnp.exp(ltot - lcum)
            ch["e_tot"] = jnp.exp(ltot)
            ch["rs"] = stack(ch["r"] * jnp.exp(lcum))
            ch["ks_b"] = stack(ch["kk"] * jnp.exp(lcum - ch["lw"])).astype(BF16)
            kbs_b = stack(ch["kd"] * e_nl).astype(BF16)
            bbs_b = stack(ch["bb"] * e_nl).astype(BF16)
            ch["kts_b"] = stack(ch["kd"] * e_rem).astype(BF16)
            ch["bts_b"] = stack(ch["bb"] * e_rem).astype(BF16)
            ch["lhs"] = jnp.concatenate([ch["ks_b"], ch["rs"].astype(BF16)], axis=0)
            ch["rhs"] = jnp.concatenate([kbs_b, bbs_b], axis=0)
        for ch in chains:
            ch["aa"] = _dot(ch["lhs"], ch["rhs"], _NT)
        for ch in chains:
            d, aa = ch["d"], ch["aa"]
            ch["a_kk"] = jnp.where(strict_m[d], aa[:c2, :c2], 0.0).astype(BF16)
            ch["a_rk"] = jnp.where(incl_m[d], aa[c2:, :c2], 0.0).astype(BF16)
            ch["a_rb"] = jnp.where(incl_m[d], aa[c2:, c2:], 0.0).astype(BF16)
            ch["a_kb"] = aa[:c2, c2:]
            ch["tinv"] = eye2 - jnp.where(level_m[d][0], ch["a_kb"], 0.0)
        for ch in chains:
            ch["av"] = _dot(ch["a_kk"], ch["vs_h"]).astype(BF16)
        for ch in chains:
            ch["vk"] = _dot(ch["vs_h"], ch["kts_b"], _TN)
        for lvl in range(1, n_levels):
            for ch in chains:
                ch["tb"] = ch["tinv"].astype(BF16)
                coupling = jnp.where(level_m[ch["d"]][lvl], ch["a_kb"], 0.0).astype(BF16)
                ch["e"] = _dot(coupling, ch["tb"]).astype(BF16)
            for ch in chains:
                ch["tinv"] = ch["tinv"] - _dot(ch["tb"], ch["e"])
        for ch in chains:
            wu = _dot(ch["tinv"].astype(BF16), jnp.concatenate([ch["ks_b"], ch["av"]], axis=1))
            ch["wu_b"] = wu.astype(BF16)
        for ch in chains:
            ch["ru"] = _dot(ch["a_rb"], ch["wu_b"])
            ch["y0"] = _dot(ch["a_rk"], ch["vs_h"])
        for ch in chains:
            ch["wb"] = _dot(ch["wu_b"], ch["bts_b"], _TN)
        y0_sum = {}
        for ch in chains:
            d, n, rows, sq = ch["d"], ch["n"], ch["rows"], ch["sq"]
            rt = ch["rs"] - ch["ru"][:, :LANES]
            y0 = ch["y0"] - ch["ru"][:, LANES:]
            gmat = jnp.where(diag_l, ch["e_tot"], 0.0) - jnp.where(block_l, ch["wb"][:LANES], 0.0)
            dmat = jnp.where(block_l, ch["vk"] - ch["wb"][LANES:], 0.0)
            rt_s[sq, d, rows, :] = unstack(rt).astype(BF16)
            g_s[sq, d, n] = gmat.astype(BF16)
            ds_s[sq, d, n] = dmat
            if d == 0:
                y0_sum = unstack(y0)
            else:
                ys_s[sq, rows, :] = y0_sum + unstack(y0)
        return carry

    total_chunks = nseq * n_chunks
    unroll = 8 if total_chunks % 8 == 0 else (4 if total_chunks % 4 == 0 else (2 if total_chunks % 2 == 0 else 1))
    assert nseq == 1 or unroll % n_chunks == 0
    lax.fori_loop(0, total_chunks // unroll, phase_a, 0)

    for sq in range(nseq):
        for d in range(2):
            st_s[sq, d] = s0_ref[sq, d, 0]
    lng = lng_ref[...]
    lnb = lnb_ref[...]
    rkw = rk_ref[...]
    g2 = g2_ref[...]

    def finish(sq, rows):
        y = ys_s[sq, rows, :]
        mean = head_sum(y) * (1.0 / hn)
        yc = y - mean
        var = head_sum(yc * yc) * (1.0 / hn)
        yn = yc * lax.rsqrt(var + RWKV_LN_EPS) * lng + lnb
        r = rkv_s[sq, 0, rows, :]
        k = rkv_s[sq, 1, rows, :]
        v = rkv_s[sq, 2, rows, :]
        bonus = head_sum(r * k * rkw) * v
        gate = _dot(_sigmoid(l1g_ref[sq, rows, :]).astype(BF16), g2)
        o_ref[sq, rows, :] = ((yn + bonus) * gate).astype(o_ref.dtype)

    def chunk_rows(idx):
        return pl.ds(pl.multiple_of(idx * c, c), c)

    def state_step(i):
        for sq in range(nseq):
            for d in range(2):
                idx = i if d == 0 else n_chunks - 1 - i
                rows = chunk_rows(idx)
                st_b = st_s[sq, d].astype(BF16)
                ys_s[sq, rows, :] = ys_s[sq, rows, :] + _dot(rt_s[sq, d, rows, :], st_b, _NT)
                st_s[sq, d] = _dot(st_b, g_s[sq, d, idx]) + ds_s[sq, d, idx]

    def phase_b_first(i, carry):
        state_step(i)
        return carry

    def phase_b_second(i, carry):
        for sq in range(nseq):
            finish(sq, chunk_rows(i - 1))
            finish(sq, chunk_rows(n_chunks - i))
        state_step(i)
        return carry

    assert n_chunks % 2 == 0
    half = n_chunks // 2
    lax.fori_loop(0, half + 1, phase_b_first, 0)
    lax.fori_loop(half + 1, n_chunks, phase_b_second, 0)
    for sq in range(nseq):
        finish(sq, chunk_rows(n_chunks - 1))
        finish(sq, chunk_rows(0))
    if want_sf:
        for sq in range(nseq):
            for d in range(2):
                sf_ref[sq, d, 0] = st_s[sq, d]


def _rwkv(proj, w2p, a2p, g2, w0, a0, mu, kkw, ka, rkw, lng, lnb, s0_bd, want_sf, col_rkv, col_l1, col_g):
    b, t, _ = proj.shape
    width = kkw.shape[-1]
    pairs = width // LANES
    c = RWKV_CHUNK
    n_chunks = t // c
    nseq = 1
    for target in (16, 8):
        if n_chunks < target and target % n_chunks == 0 and b % (target // n_chunks) == 0:
            nseq = target // n_chunks
            break
    cb = col_rkv // LANES
    vec = lambda: pl.BlockSpec((1, LANES), lambda bi, h: (0, h))
    in_specs = [
        pl.BlockSpec((nseq, t, LANES), lambda bi, h: (bi, 0, cb + h)),
        pl.BlockSpec((nseq, t, LANES), lambda bi, h: (bi, 0, cb + pairs + h)),
        pl.BlockSpec((nseq, t, LANES), lambda bi, h: (bi, 0, cb + 2 * pairs + h)),
        pl.BlockSpec((nseq, t, 4 * LANES), lambda bi, h: (bi, 0, col_l1 // (4 * LANES))),
        pl.BlockSpec((nseq, t, 2 * LANES), lambda bi, h: (bi, 0, col_g // (2 * LANES))),
        pl.BlockSpec((2, LANES, LANES), lambda bi, h: (0, 0, h)),
        pl.BlockSpec((2, LANES, LANES), lambda bi, h: (0, 0, h)),
        pl.BlockSpec((2 * LANES, LANES), lambda bi, h: (0, h)),
        pl.BlockSpec((2, 1, LANES), lambda bi, h: (0, 0, h)),
        pl.BlockSpec((2, 1, LANES), lambda bi, h: (0, 0, h)),
        pl.BlockSpec((2, 3, 1, LANES), lambda bi, h: (0, 0, 0, h)),
        vec(), vec(), vec(), vec(), vec(),
        pl.BlockSpec((nseq, 2, 1, LANES, LANES), lambda bi, h: (bi, 0, h, 0, 0)),
    ]
    out_specs = [pl.BlockSpec((nseq, t, LANES), lambda bi, h: (bi, 0, h))]
    out_shape = [jax.ShapeDtypeStruct((b, t, width), BF16)]
    if want_sf:
        out_specs.append(pl.BlockSpec((nseq, 2, 1, LANES, LANES), lambda bi, h: (bi, 0, h, 0, 0)))
        out_shape.append(jax.ShapeDtypeStruct((b, 2, pairs, LANES, LANES), F32))
    res = pl.pallas_call(
        functools.partial(_rwkv_kernel, seq=t, chunk=c, nseq=nseq, want_sf=want_sf),
        grid=(b // nseq, pairs),
        in_specs=in_specs,
        out_specs=out_specs,
        out_shape=out_shape,
        scratch_shapes=[pltpu.VMEM((nseq, 3, t, LANES), F32),
                        pltpu.VMEM((nseq, 2, t, LANES), BF16),
                        pltpu.VMEM((nseq, t, LANES), F32),
                        pltpu.VMEM((nseq, 2, n_chunks, LANES, LANES), BF16),
                        pltpu.VMEM((nseq, 2, n_chunks, LANES, LANES), F32),
                        pltpu.VMEM((nseq, 2, LANES, LANES), F32)],
        compiler_params=_cparams(2),
        name="rwkv7_chunked",
    )(proj, proj, proj, proj, proj, w2p, a2p, g2, w0, a0, mu, kkw, ka, rkw, lng, lnb, s0_bd)
    return res if want_sf else (res[0], None)


def _rope(x, cos, sin, first_half):
    part = jnp.where(first_half, pltpu.roll(x, x.shape[-1] - 32, 1), pltpu.roll(x, 32, 1))
    return x * cos + part * sin


def _diff_attn_kernel(*refs, rope, has_cache, lam_init, tq):
    q_ref, k_ref, v_ref, lq_ref, lk_ref, sub_ref = refs[:6]
    pos = 6
    if rope:
        cos_ref, sin_ref = refs[pos:pos + 2]
        pos += 2
    if has_cache:
        ck_ref, cv_ref = refs[pos:pos + 2]
        pos += 2
    o_ref = refs[pos]
    ks_ref, vs_ref = refs[pos + 1:pos + 3]

    t = k_ref.shape[2]
    w = k_ref.shape[3]
    dh = DIFF_DH
    i = pl.program_id(2)
    lane = lax.broadcasted_iota(jnp.int32, (1, w), 1)
    first_half = (lane & 63) < 32

    @pl.when(i == 0)
    def _():
        kb = min(t, 512)

        def fill(j, carry):
            rows = pl.ds(pl.multiple_of(j * kb, kb), kb)
            kx = k_ref[0, 0, rows, :]
            if rope:
                kx = _rope(kx, cos_ref[rows, :], sin_ref[rows, :], first_half)
            ks_ref[rows, :] = kx.astype(BF16)
            vs_ref[rows, :] = v_ref[0, 0, rows, :].astype(BF16)
            return carry

        lax.fori_loop(0, t // kb, fill, 0)
        if has_cache:
            ks_ref[t:, :] = ck_ref[0, 0, 0].astype(BF16)
            vs_ref[t:, :] = cv_ref[0, 0, 0].astype(BF16)

    q = q_ref[0, 0]
    if rope:
        qrows = pl.ds(pl.multiple_of(i * tq, tq), tq)
        q = _rope(q, cos_ref[qrows, :], sin_ref[qrows, :], first_half)
    qb = q.astype(BF16)
    lq = lq_ref[...]
    lk = lk_ref[...]
    lam = (jnp.exp(jnp.sum(lq[0:1] * lk[0:1], axis=-1, keepdims=True))
           - jnp.exp(jnp.sum(lq[1:2] * lk[1:2], axis=-1, keepdims=True)) + lam_init)
    kall = ks_ref[...]
    vall = vs_ref[...]
    exp2_scale = dh ** -0.5 * math.log2(math.e)
    sub = min(tq, 128)
    scores = [[_dot(qb[r:r + sub, m * dh:(m + 1) * dh], kall[:, m * dh:(m + 1) * dh], _NT) for m in range(2)]
              for r in range(0, tq, sub)]
    rows_out = []
    for pair in scores:
        es, invs = [], []
        for s in pair:
            e = jnp.exp2((s - jnp.max(s, axis=-1, keepdims=True)) * exp2_scale)
            es.append(e)
            invs.append(1.0 / jnp.sum(e, axis=-1, keepdims=True))
        p = es[0] * invs[0] - es[1] * (lam * invs[1])
        rows_out.append(_dot(p.astype(BF16), vall))
    o = jnp.concatenate(rows_out, axis=0) if len(rows_out) > 1 else rows_out[0]
    on = o * lax.rsqrt(jnp.mean(o * o, axis=-1, keepdims=True) + SUBLN_EPS) * sub_ref[...]
    o_ref[0] = (on * (1.0 - lam_init)).astype(o_ref.dtype)


def _diff_attention(q, k, v, lq, lk, subln, lam_init, rope_tabs, cache_k, cache_v, tq=512):
    b, heads, t, w = q.shape
    tq = _pick(t, tq)
    rope = rope_tabs is not None
    has_cache = cache_k is not None
    tk = t + (cache_k.shape[3] if has_cache else 0)
    in_specs = [
        pl.BlockSpec((1, 1, tq, w), lambda bi, h, i: (bi, h, i, 0)),
        pl.BlockSpec((1, 1, t, w), lambda bi, h, i: (bi, h, 0, 0)),
        pl.BlockSpec((1, 1, t, w), lambda bi, h, i: (bi, h, 0, 0)),
        pl.BlockSpec((2, DIFF_DH), lambda bi, h, i: (0, 0)),
        pl.BlockSpec((2, DIFF_DH), lambda bi, h, i: (0, 0)),
        pl.BlockSpec((1, w), lambda bi, h, i: (0, 0)),
    ]
    args = [q, k, v, lq, lk, subln.reshape(1, w)]
    if rope:
        in_specs += [pl.BlockSpec((t, w), lambda bi, h, i: (0, 0))] * 2
        args += list(rope_tabs)
    if has_cache:
        pc = cache_k.shape[3]
        in_specs += [pl.BlockSpec((1, 1, 1, pc, w), lambda bi, h, i: (bi, 0, h, 0, 0))] * 2
        args += [cache_k, cache_v]
    return pl.pallas_call(
        functools.partial(_diff_attn_kernel, rope=rope, has_cache=has_cache, lam_init=lam_init, tq=tq),
        grid=(b, heads, t // tq),
        in_specs=in_specs,
        out_specs=pl.BlockSpec((1, tq, w), lambda bi, h, i: (bi, i, h)),
        out_shape=jax.ShapeDtypeStruct((b, t, heads * w), BF16),
        scratch_shapes=[pltpu.VMEM((tk, w), BF16), pltpu.VMEM((tk, w), BF16)],
        compiler_params=_cparams(3),
        name="diff_attention",
    )(*args)


def _rope_tables(t, w):
    nf = DIFF_DH // 4
    rows = t // GRID_W
    row = jnp.repeat(jnp.arange(rows, dtype=F32), GRID_W)
    col = jnp.tile(jnp.arange(GRID_W, dtype=F32), rows)
    inv = jnp.power(ROPE_BASE, -jnp.arange(nf, dtype=F32) / nf)
    ang_r = row[:, None] * inv
    ang_c = col[:, None] * inv
    cos = jnp.concatenate([jnp.cos(ang_r), jnp.cos(ang_r), jnp.cos(ang_c), jnp.cos(ang_c)], axis=-1)
    sin = jnp.concatenate([-jnp.sin(ang_r), jnp.sin(ang_r), -jnp.sin(ang_c), jnp.sin(ang_c)], axis=-1)
    reps = w // DIFF_DH
    return jnp.tile(cos, (1, reps)), jnp.tile(sin, (1, reps))


def _pad_rows(w, rows):
    return jnp.pad(w, ((0, 0),) * (w.ndim - 2) + ((0, rows - w.shape[-2]), (0, 0)))


def _pad_cols(w, cols):
    return jnp.pad(w, ((0, 0),) * (w.ndim - 1) + ((0, cols - w.shape[-1]),))


def kernel(x_prompt, x_sample, state_gla, state_rwkv, cache_k, cache_v, c, c_ctx, ada_w, ada_b, norm_mix, norm_ffn, norm_final, ffn_w1, ffn_w3, ffn_w2, even_w_in, even_w_out, gla_a1, gla_a2, gla_ab, gla_norm, rw_mu, rw_w0, rw_w1, rw_w2, rw_a0, rw_a1, rw_a2, rw_g1, rw_g2, rw_kk, rw_ka, rw_rk, rw_ln_g, rw_ln_b, odd_w_in, odd_w_out, diff_lq, diff_lk, diff_subln):
    depth, d, _ = ada_w.shape
    dec_b = x_sample.shape[0]
    gla_heads, gla_dk, gla_dv = state_gla.shape[3:]
    rw_heads = state_rwkv.shape[3]
    rw_w = rw_heads * RWKV_N
    pairs = rw_w // LANES
    diff_heads, _, diff_w = cache_k.shape[2:]
    gla_qk = gla_heads * gla_dk
    gla_vw = gla_heads * gla_dv
    even_in = even_w_in.shape[-1]

    nb = -(-(1 + dec_b) // 8) * 8
    cond = jnp.zeros((nb, d), F32).at[0].set(c_ctx).at[1:1 + dec_b].set(c)
    mod = _adaln(cond, ada_w, ada_b)

    def mods(layer, prompt):
        m = mod[layer, 0:1] if prompt else mod[layer, 1:1 + dec_b]
        return [m[:, None, k * d:(k + 1) * d] for k in range(6)]

    def even_weights(e):
        lora1 = jnp.concatenate(
            [_pad_cols(rw_w1[e, 0], LANES), _pad_cols(rw_w1[e, 1], LANES),
             _pad_cols(rw_a1[e, 0], LANES), _pad_cols(rw_a1[e, 1], LANES),
             rw_g1[e],
             _pad_cols(gla_a1[e, 0], LANES), _pad_cols(gla_a1[e, 1], LANES)], axis=-1)
        w_cat = jnp.concatenate([even_w_in[e], lora1], axis=-1).astype(BF16)
        return dict(
            w_cat=w_cat,
            w_out=even_w_out[e].astype(BF16),
            gla_a2=_pad_rows(gla_a2[e], LANES),
            gla_ab=gla_ab[e].reshape(2, 1, gla_qk),
            gla_norm=gla_norm[e].reshape(1, gla_dv),
            w2=_pad_rows(rw_w2[e], LANES).astype(BF16),
            a2=_pad_rows(rw_a2[e], LANES).astype(BF16),
            g2=rw_g2[e].astype(BF16),
            w0=rw_w0[e].reshape(2, 1, rw_w),
            a0=rw_a0[e].reshape(2, 1, rw_w),
            mu=rw_mu[e].reshape(2, 3, 1, rw_w),
            kk=rw_kk[e].reshape(1, rw_w), ka=rw_ka[e].reshape(1, rw_w), rk=rw_rk[e].reshape(1, rw_w),
            lng=rw_ln_g[e].reshape(1, rw_w), lnb=rw_ln_b[e].reshape(1, rw_w),
        )

    col_rkv = 2 * gla_qk + 2 * gla_vw
    col_l1 = even_in
    col_g = even_in + 4 * LANES
    col_gl = even_in + 6 * LANES

    def block_diag_state(s):
        bsz = s.shape[0]
        s = s.reshape(bsz, 2, pairs, 2, RWKV_N, RWKV_N)
        z = jnp.zeros_like(s[:, :, :, 0])
        top = jnp.concatenate([s[:, :, :, 0], z], axis=-1)
        bot = jnp.concatenate([z, s[:, :, :, 1]], axis=-1)
        return jnp.concatenate([top, bot], axis=-2)

    def diag_blocks(sbd):
        bsz = sbd.shape[0]
        h0 = sbd[:, :, :, :RWKV_N, :RWKV_N]
        h1 = sbd[:, :, :, RWKV_N:, RWKV_N:]
        return jnp.stack([h0, h1], axis=3).reshape(bsz, 2, rw_heads, RWKV_N, RWKV_N)

    def even_mixer(h, ew, s_gla, s_rwkv, want_states):
        proj = _matmul(h, ew["w_cat"], tn=1792, name="even_in_proj")
        o_gla, sg = _gla(proj, ew["gla_a2"], ew["gla_ab"], ew["gla_norm"], s_gla, want_states,
                         gla_heads, gla_dk, gla_dv, col_gl)
        if s_rwkv is None:
            s_bd = jnp.zeros((h.shape[0], 2, pairs, LANES, LANES), F32)
        else:
            s_bd = block_diag_state(s_rwkv)
        y_rw, sr = _rwkv(proj, ew["w2"], ew["a2"], ew["g2"], ew["w0"], ew["a0"], ew["mu"], ew["kk"], ew["ka"],
                         ew["rk"], ew["lng"], ew["lnb"], s_bd, want_states, col_rkv, col_l1, col_g)
        mix = [o_gla, y_rw]
        return mix, ew["w_out"], sg, (diag_blocks(sr) if want_states else None)

    def run_group(x, prompt):
        outs = {}
        for layer in range(depth):
            sh1, sc1, g1, sh2, sc2, g2 = mods(layer, prompt)
            h = _norm_mod(x, norm_mix[layer], sc1, sh1)
            if layer % 2 == 0:
                e = layer // 2
                ew = even_weights(e)
                if prompt:
                    mix, w_out, sg, sr = even_mixer(h, ew, None, None, True)
                    outs.setdefault("gla", []).append(sg)
                    outs.setdefault("rwkv", []).append(sr)
                else:
                    mix, w_out, _, _ = even_mixer(h, ew, state_gla[:, e], state_rwkv[:, e], False)
            else:
                o = layer // 2
                lam_init = 0.8 - 0.6 * math.exp(-0.3 * layer)
                q, k, v = _matmul_heads(h, odd_w_in[o].astype(BF16), diff_heads, diff_w)
                if prompt:
                    outs.setdefault("k", []).append(k)
                    outs.setdefault("v", []).append(v)
                    mix = _diff_attention(q, k, v, diff_lq[o], diff_lk[o], diff_subln[o], lam_init,
                                          None, None, None)
                else:
                    mix = _diff_attention(q, k, v, diff_lq[o], diff_lk[o], diff_subln[o], lam_init,
                                          _rope_tables(x.shape[1], diff_w), cache_k[:, o:o + 1], cache_v[:, o:o + 1])
                w_out = odd_w_out[o].astype(BF16)
            x = _matmul_residual(mix, w_out, x, g1, tm=512, tn=2048, name="mix_out_proj")
            h = _norm_mod(x, norm_ffn[layer], sc2, sh2)
            u = _matmul_swiglu(h, ffn_w1[layer].astype(BF16), ffn_w3[layer].astype(BF16))
            x = _matmul_residual(u, ffn_w2[layer].astype(BF16), x, g2, name="ffn_down_proj")
        return _final_norm(x, norm_final), outs

    y_prompt, po = run_group(x_prompt, True)
    y_sample, _ = run_group(x_sample, False)
    return (y_prompt, y_sample,
            jnp.stack(po["gla"], axis=1), jnp.stack(po["rwkv"], axis=1),
            jnp.stack(po["k"], axis=1), jnp.stack(po["v"], axis=1))
```
